```python
import math, functools
import jax, jax.numpy as jnp
from jax import lax
import numpy as np

D_MODEL = 1024
BATCH = 32
SEQ = 2048
DEPTH = 1
DEC_BATCH = 128
DEC_SEQ = 1
PAST_LEN = 8192
PAGE_SIZE = 128

SSM_D_INNER = D_MODEL
SSM_HEAD_DIM = 64
SSM_HEADS = SSM_D_INNER // SSM_HEAD_DIM
SSM_GROUPS = 2
SSM_HPG = SSM_HEADS // SSM_GROUPS
SSM_STATE = 64
SSM_CONV = 4
SSM_CHUNK = 128
CONV_DIM = SSM_D_INNER + 2 * SSM_GROUPS * SSM_STATE
ATT_HEADS = 8
KV_HEADS = 4
KV_REP = ATT_HEADS // KV_HEADS
QK_DIM = 64
V_DIM = 2 * QK_DIM
ATT_WIDTH = ATT_HEADS * V_DIM
ROT_DIM = QK_DIM // 4
ROPE_THETA = 500000.0
Q_BLOCK = 128
D_FF = 2816
FFN_CONV = 3
EPS = 1e-6
IN_DIM = SSM_D_INNER + CONV_DIM + SSM_HEADS + ATT_HEADS * 2 * QK_DIM + KV_HEADS * 2 * QK_DIM + KV_HEADS * V_DIM + 2 * D_MODEL

kernel_name = 'hybrid_ssd_diffattn_convffn_step'

F32 = jnp.float32


def rms_norm(x, g):
    xf = x.astype(F32)
    y = xf * lax.rsqrt(jnp.mean(xf * xf, axis=-1, keepdims=True) + EPS)
    return (y * g.astype(F32)).astype(x.dtype)


def rope_partial(x, pos):
    half = ROT_DIM // 2
    inv = ROPE_THETA ** (-jnp.arange(half, dtype=F32) * 2.0 / ROT_DIM)
    ang = pos.astype(F32)[:, None] * inv[None, :]
    shp = (1, pos.shape[0]) + (1,) * (x.ndim - 3) + (half,)
    cos = jnp.cos(ang).reshape(shp)
    sin = jnp.sin(ang).reshape(shp)
    xf = x.astype(F32)
    x1, x2, rest = xf[..., :half], xf[..., half:ROT_DIM], xf[..., ROT_DIM:]
    out = jnp.concatenate([x1 * cos - x2 * sin, x2 * cos + x1 * sin, rest], axis=-1)
    return out.astype(x.dtype)


def causal_dwconv(u, buf, w, bias):
    full = jnp.concatenate([buf.astype(u.dtype), u], axis=1)
    L = u.shape[1]
    W = w.shape[0]
    out = bias
    for k in range(W):
        out = out + full[:, k:k + L] * w[k]
    return out, full[:, full.shape[1] - (W - 1):]


def split_in_proj(h, w_in):
    u = jnp.einsum('bld,de->ble', h, w_in)
    sizes = (SSM_D_INNER, CONV_DIM, SSM_HEADS, ATT_HEADS * 2 * QK_DIM,
             KV_HEADS * 2 * QK_DIM, KV_HEADS * V_DIM, D_MODEL, D_MODEL)
    cuts = np.cumsum(sizes)[:-1].tolist()
    return jnp.split(u, cuts, axis=-1)


def ssd_scan(xs, dt, a, bm, cm, h0, chunk):
    b, L, G, R, P = xs.shape
    N = bm.shape[-1]
    nc = L // chunk
    xdt = (xs.astype(F32) * dt[..., None]).reshape(b, nc, chunk, G, R, P)
    da = (dt * a).reshape(b, nc, chunk, G, R)
    bm = bm.astype(F32).reshape(b, nc, chunk, G, N)
    cm = cm.astype(F32).reshape(b, nc, chunk, G, N)
    cs = jnp.cumsum(da, axis=2)
    causal = jnp.tril(jnp.ones((chunk, chunk), bool))[None, None, :, :, None, None]
    seg = cs[:, :, :, None] - cs[:, :, None, :]
    decay = jnp.exp(jnp.where(causal, seg, -jnp.inf))
    cb = jnp.einsum('bclgn,bcsgn->bclsg', cm, bm)
    y_diag = jnp.einsum('bclsgr,bcsgrp->bclgrp', cb[..., None] * decay, xdt)
    to_end = jnp.exp(cs[:, :, -1:] - cs)
    chunk_states = jnp.einsum('bclgn,bclgrp->bcgrpn', bm, xdt * to_end[..., None])
    chunk_decay = jnp.exp(cs[:, :, -1])

    def step(h, inp):
        st, dec = inp
        return h * dec[..., None, None] + st, h

    h_last, h_prev = lax.scan(step, h0.astype(F32),
                              (jnp.moveaxis(chunk_states, 1, 0), jnp.moveaxis(chunk_decay, 1, 0)))
    h_prev = jnp.moveaxis(h_prev, 0, 1)
    y_off = jnp.einsum('bclgn,bcgrpn->bclgrp', cm, h_prev) * jnp.exp(cs)[..., None]
    y = (y_diag + y_off).reshape(b, L, G, R, P).astype(xs.dtype)
    return y, h_last.astype(h0.dtype)


def ssd_branch(z, xbc, dt_raw, conv_buf, h0, chunk, conv_w, conv_b, dt_bias, a_log, d_skip, norm_g):
    b, L = z.shape[:2]
    xbc, new_buf = causal_dwconv(xbc, conv_buf, conv_w, conv_b)
    xbc = jax.nn.silu(xbc)
    xs, bm, cm = jnp.split(xbc, [SSM_D_INNER, SSM_D_INNER + SSM_GROUPS * SSM_STATE], axis=-1)
    xs = xs.reshape(b, L, SSM_GROUPS, SSM_HPG, SSM_HEAD_DIM)
    bm = bm.reshape(b, L, SSM_GROUPS, SSM_STATE)
    cm = cm.reshape(b, L, SSM_GROUPS, SSM_STATE)
    dt = jax.nn.softplus(dt_raw.astype(F32) + dt_bias.astype(F32)).reshape(b, L, SSM_GROUPS, SSM_HPG)
    a = -jnp.exp(a_log.astype(F32)).reshape(SSM_GROUPS, SSM_HPG)
    y, h_new = ssd_scan(xs, dt, a, bm, cm, h0, chunk)
    y = y + xs * d_skip.reshape(SSM_GROUPS, SSM_HPG)[:, :, None]
    gw = SSM_D_INNER // SSM_GROUPS
    y = y.reshape(b, L, SSM_GROUPS, gw) * jax.nn.silu(z).reshape(b, L, SSM_GROUPS, gw)
    y = rms_norm(y, norm_g.reshape(SSM_GROUPS, gw))
    return y.reshape(b, L, SSM_D_INNER), new_buf, h_new


def diff_probs(s, lam):
    p = jax.nn.softmax(s, axis=-1)
    return p[..., 0, :, :] - lam * p[..., 1, :, :]


def prompt_diff_attn(q, k, v, lam):
    b, L = q.shape[:2]
    nb = L // Q_BLOCK
    scale = QK_DIM ** -0.5
    qb = jnp.moveaxis(q.reshape(b, nb, Q_BLOCK, KV_HEADS, KV_REP, 2, QK_DIM), 1, 0)
    kpos = jnp.arange(L)

    def block(args):
        qi, i = args
        s = jnp.einsum('bqkrcd,btkcd->bkrcqt', qi, k).astype(F32) * scale
        qpos = i * Q_BLOCK + jnp.arange(Q_BLOCK)
        s = jnp.where(kpos[None, :] <= qpos[:, None], s, -jnp.inf)
        a = diff_probs(s, lam).astype(v.dtype)
        return jnp.einsum('bkrqt,btkv->bqkrv', a, v)

    out = lax.map(block, (qb, jnp.arange(nb)))
    return jnp.moveaxis(out, 0, 1).reshape(b, L, KV_HEADS, KV_REP, V_DIM)


def sample_diff_attn(q, k, v, lam, cache_k, cache_v, page_table, layer):
    b, S = q.shape[:2]
    past = page_table.shape[1] * PAGE_SIZE
    kp = cache_k[layer, page_table].reshape(b, past, KV_HEADS, 2, QK_DIM)
    vp = cache_v[layer, page_table].reshape(b, past, KV_HEADS, V_DIM)
    scale = QK_DIM ** -0.5
    s_past = jnp.einsum('bqkrcd,btkcd->bkrcqt', q, kp).astype(F32) * scale
    s_new = jnp.einsum('bqkrcd,btkcd->bkrcqt', q, k).astype(F32) * scale
    s_new = jnp.where(jnp.tril(jnp.ones((S, S), bool)), s_new, -jnp.inf)
    a = diff_probs(jnp.concatenate([s_past, s_new], axis=-1), lam).astype(v.dtype)
    return (jnp.einsum('bkrqt,btkv->bqkrv', a[..., :past], vp)
            + jnp.einsum('bkrqt,btkv->bqkrv', a[..., past:], v))


def trunk_layer(x, pos, attend, chunk, conv_buf, ssm_h0, ffn_buf, lam_init, *,
                norm_mix_g, w_in, ssm_conv_w, ssm_conv_b, ssm_dt_bias, ssm_a_log, ssm_d, ssm_norm_g,
                q_norm_g, k_norm_g, lambda_q1, lambda_k1, lambda_q2, lambda_k2, attn_subln_g,
                w_ssd_o, w_attn_o, w_o, norm_ffn_g, w_ffn_up, ffn_conv_w, ffn_conv_b, w_ffn_down):
    b, L, _ = x.shape
    h = rms_norm(x, norm_mix_g)
    z, xbc, dt_raw, q, k, v, g_ssd, g_att = split_in_proj(h, w_in)
    y_ssd, new_conv, new_h = ssd_branch(z, xbc, dt_raw, conv_buf, ssm_h0, chunk, ssm_conv_w, ssm_conv_b,
                                        ssm_dt_bias, ssm_a_log, ssm_d, ssm_norm_g)
    q = rope_partial(rms_norm(q.reshape(b, L, KV_HEADS, KV_REP, 2, QK_DIM), q_norm_g), pos)
    k = rope_partial(rms_norm(k.reshape(b, L, KV_HEADS, 2, QK_DIM), k_norm_g), pos)
    v = v.reshape(b, L, KV_HEADS, V_DIM)
    lam = (jnp.exp(jnp.sum(lambda_q1.astype(F32) * lambda_k1.astype(F32)))
           - jnp.exp(jnp.sum(lambda_q2.astype(F32) * lambda_k2.astype(F32))) + lam_init)
    o = attend(q, k, v, lam)
    y_att = (rms_norm(o, attn_subln_g) * (1.0 - lam_init)).reshape(b, L, ATT_WIDTH)
    merged = (jax.nn.sigmoid(g_ssd) * jnp.einsum('ble,ed->bld', y_ssd, w_ssd_o)
              + jax.nn.sigmoid(g_att) * jnp.einsum('ble,ed->bld', y_att, w_attn_o))
    x = x + jnp.einsum('bld,de->ble', merged, w_o)
    u = jnp.einsum('bld,df->blf', rms_norm(x, norm_ffn_g), w_ffn_up)
    u, new_ffn = causal_dwconv(u, ffn_buf, ffn_conv_w, ffn_conv_b)
    ua, ug = jnp.split(u, 2, axis=-1)
    x = x + jnp.einsum('blf,fd->bld', jax.nn.silu(ug) * ua, w_ffn_down)
    k_rows = k.reshape(b, L, KV_HEADS, 2 * QK_DIM)
    return x, k_rows, v, new_conv, new_h, new_ffn


def setup_inputs(seed: int = 0) -> dict:
    key = jax.random.key(seed)
    ks = iter(jax.random.split(key, 48))

    def nrm(shape, scale):
        return jax.random.normal(next(ks), shape, F32) * scale

    def gain(n):
        return 1.0 + nrm((DEPTH, n), 0.02)

    n_pages = PAST_LEN // PAGE_SIZE
    n_pool = (DEC_BATCH * n_pages * 5) // 4
    inp = {}
    inp['x_prompt'] = nrm((BATCH, SEQ, D_MODEL), 1.0)
    inp['x_sample'] = nrm((DEC_BATCH, DEC_SEQ, D_MODEL), 1.0)
    inp['cache_k'] = nrm((DEPTH, n_pool, PAGE_SIZE, KV_HEADS, 2 * QK_DIM), 1.0)
    inp['cache_v'] = nrm((DEPTH, n_pool, PAGE_SIZE, KV_HEADS, V_DIM), 1.0)
    inp['state_conv'] = nrm((DEPTH, DEC_BATCH, SSM_CONV - 1, CONV_DIM), 1.0)
    inp['state_ssm'] = nrm((DEPTH, DEC_BATCH, SSM_GROUPS, SSM_HPG, SSM_HEAD_DIM, SSM_STATE), 0.1)
    inp['state_ffn_conv'] = nrm((DEPTH, DEC_BATCH, FFN_CONV - 1, 2 * D_FF), 1.0)
    perm = jax.random.permutation(next(ks), n_pool)[: DEC_BATCH * n_pages]
    inp['page_table'] = perm.reshape(DEC_BATCH, n_pages).astype(jnp.int32)
    inp['norm_mix_g'] = gain(D_MODEL)
    inp['w_in'] = nrm((DEPTH, D_MODEL, IN_DIM), D_MODEL ** -0.5)
    inp['ssm_conv_w'] = nrm((DEPTH, SSM_CONV, CONV_DIM), SSM_CONV ** -0.5)
    inp['ssm_conv_b'] = nrm((DEPTH, CONV_DIM), 0.02)
    dt0 = jnp.exp(jax.random.uniform(next(ks), (DEPTH, SSM_HEADS), F32, math.log(1e-3), math.log(1e-1)))
    inp['ssm_dt_bias'] = dt0 + jnp.log(-jnp.expm1(-dt0))
    inp['ssm_a_log'] = jnp.log(jax.random.uniform(next(ks), (DEPTH, SSM_HEADS), F32, 1.0, 16.0))
    inp['ssm_d'] = 1.0 + nrm((DEPTH, SSM_HEADS), 0.1)
    inp['ssm_norm_g'] = gain(SSM_D_INNER)
    inp['q_norm_g'] = gain(QK_DIM)
    inp['k_norm_g'] = gain(QK_DIM)
    inp['lambda_q1'] = nrm((DEPTH, QK_DIM), 0.1)
    inp['lambda_k1'] = nrm((DEPTH, QK_DIM), 0.1)
    inp['lambda_q2'] = nrm((DEPTH, QK_DIM), 0.1)
    inp['lambda_k2'] = nrm((DEPTH, QK_DIM), 0.1)
    inp['attn_subln_g'] = gain(V_DIM)
    inp['w_ssd_o'] = nrm((DEPTH, SSM_D_INNER, D_MODEL), SSM_D_INNER ** -0.5)
    inp['w_attn_o'] = nrm((DEPTH, ATT_WIDTH, D_MODEL), ATT_WIDTH ** -0.5)
    inp['w_o'] = nrm((DEPTH, D_MODEL, D_MODEL), D_MODEL ** -0.5)
    inp['norm_ffn_g'] = gain(D_MODEL)
    inp['w_ffn_up'] = nrm((DEPTH, D_MODEL, 2 * D_FF), D_MODEL ** -0.5)
    inp['ffn_conv_w'] = nrm((DEPTH, FFN_CONV, 2 * D_FF), FFN_CONV ** -0.5)
    inp['ffn_conv_b'] = nrm((DEPTH, 2 * D_FF), 0.02)
    inp['w_ffn_down'] = nrm((DEPTH, D_FF, D_MODEL), D_FF ** -0.5)
    return inp


def reference(x_prompt, x_sample, cache_k, cache_v, state_conv, state_ssm, state_ffn_conv, page_table,
              norm_mix_g, w_in, ssm_conv_w, ssm_conv_b, ssm_dt_bias, ssm_a_log, ssm_d, ssm_norm_g,
              q_norm_g, k_norm_g, lambda_q1, lambda_k1, lambda_q2, lambda_k2, attn_subln_g,
              w_ssd_o, w_attn_o, w_o, norm_ffn_g, w_ffn_up, ffn_conv_w, ffn_conv_b, w_ffn_down):
    b_p, s_p, _ = x_prompt.shape
    b_s, s_s, _ = x_sample.shape
    past = page_table.shape[1] * PAGE_SIZE
    pos_p = jnp.arange(s_p)
    pos_s = past + jnp.arange(s_s)
    xp, xs = x_prompt, x_sample
    outs_p, outs_s = [], []
    for layer in range(DEPTH):
        lp = dict(norm_mix_g=norm_mix_g[layer], w_in=w_in[layer], ssm_conv_w=ssm_conv_w[layer],
                  ssm_conv_b=ssm_conv_b[layer], ssm_dt_bias=ssm_dt_bias[layer], ssm_a_log=ssm_a_log[layer],
                  ssm_d=ssm_d[layer], ssm_norm_g=ssm_norm_g[layer], q_norm_g=q_norm_g[layer],
                  k_norm_g=k_norm_g[layer], lambda_q1=lambda_q1[layer], lambda_k1=lambda_k1[layer],
                  lambda_q2=lambda_q2[layer], lambda_k2=lambda_k2[layer], attn_subln_g=attn_subln_g[layer],
                  w_ssd_o=w_ssd_o[layer], w_attn_o=w_attn_o[layer], w_o=w_o[layer],
                  norm_ffn_g=norm_ffn_g[layer], w_ffn_up=w_ffn_up[layer], ffn_conv_w=ffn_conv_w[layer],
                  ffn_conv_b=ffn_conv_b[layer], w_ffn_down=w_ffn_down[layer])
        lam_init = 0.8 - 0.6 * math.exp(-0.3 * layer)
        conv0 = jnp.zeros((b_p, SSM_CONV - 1, CONV_DIM), xp.dtype)
        h0 = jnp.zeros((b_p, SSM_GROUPS, SSM_HPG, SSM_HEAD_DIM, SSM_STATE), state_ssm.dtype)
        ffn0 = jnp.zeros((b_p, FFN_CONV - 1, 2 * D_FF), xp.dtype)
        xp, *st_p = trunk_layer(xp, pos_p, prompt_diff_attn, SSM_CHUNK, conv0, h0, ffn0, lam_init, **lp)
        attend = functools.partial(sample_diff_attn, cache_k=cache_k, cache_v=cache_v,
                                   page_table=page_table, layer=layer)
        xs, *st_s = trunk_layer(xs, pos_s, attend, s_s, state_conv[layer], state_ssm[layer],
                                state_ffn_conv[layer], lam_init, **lp)
        outs_p.append(st_p)
        outs_s.append(st_s)
    k_prompt = jnp.stack([o[0] for o in outs_p])
    v_prompt = jnp.stack([o[1] for o in outs_p])
    conv_prompt = jnp.stack([o[2] for o in outs_p])
    ssm_prompt = jnp.stack([o[3] for o in outs_p])
    ffn_conv_prompt = jnp.stack([o[4] for o in outs_p])
    k_sample = jnp.stack([o[0] for o in outs_s])
    v_sample = jnp.stack([o[1] for o in outs_s])
    conv_sample = jnp.stack([o[2] for o in outs_s])
    ssm_sample = jnp.stack([o[3] for o in outs_s])
    ffn_conv_sample = jnp.stack([o[4] for o in outs_s])
    return (xp, xs, k_prompt, v_prompt, conv_prompt, ssm_prompt, ffn_conv_prompt,
            k_sample, v_sample, conv_sample, ssm_sample, ffn_conv_sample)
```

```python
import functools
import math

import jax
import jax.numpy as jnp
import numpy as np
from jax import lax
from jax.experimental import pallas as pl
from jax.experimental.pallas import tpu as pltpu

F32 = jnp.float32
BF16 = jnp.bfloat16
HIGHEST = lax.Precision.HIGHEST

EPS = 1e-6
PAGE_SIZE = 128
SSM_HEAD_DIM = 64
SSM_GROUPS = 2
SSM_STATE = 64
SSM_CONV = 4
SSM_CHUNK = 128
KV_HEADS = 4
KV_REP = 2
QK_DIM = 64
V_DIM = 128
ROT_DIM = QK_DIM // 4
ROPE_THETA = 500000.0
FFN_CONV = 3
QK_SCALE = QK_DIM ** -0.5
LANES = 128
SUBLANES = 8
PAGES_PER_STEP = 8
MIB = 1024 * 1024

NT_DIMS = (((1,), (1,)), ((), ()))
TN_DIMS = (((0,), (0,)), ((), ()))


def _params(semantics, vmem_mib):
    return pltpu.CompilerParams(dimension_semantics=semantics, vmem_limit_bytes=vmem_mib * MIB)


def _resident(shape):
    nd = len(shape)
    return pl.BlockSpec(shape, lambda *_: (0,) * nd, pipeline_mode=pl.Buffered(1))


def _sigmoid(x):
    return 1.0 / (1.0 + jnp.exp(-x))


def _silu(x):
    return x * _sigmoid(x)


def _softplus(x):
    return jnp.maximum(x, 0.0) + jnp.log1p(jnp.exp(-jnp.abs(x)))


def _rms(x, axis=-1):
    return x * lax.rsqrt(jnp.mean(x * x, axis=axis, keepdims=True) + EPS)


def _inproj_body(x_ref, g_ref, w_ref, *out_refs):
    h = (_rms(x_ref[...]) * g_ref[...]).astype(BF16)
    off = 0
    for ref in out_refs:
        n = ref.shape[-1]
        ref[...] = jnp.dot(h, w_ref[:, off:off + n], preferred_element_type=F32)
        off += n


def _in_proj(x, g, w, widths, tm):
    rows, d = x.shape
    return pl.pallas_call(
        _inproj_body,
        grid=(rows // tm,),
        in_specs=[pl.BlockSpec((tm, d), lambda i: (i, 0)), _resident(g.shape), _resident(w.shape)],
        out_specs=[pl.BlockSpec((tm, n), lambda i: (i, 0)) for n in widths],
        out_shape=[jax.ShapeDtypeStruct((rows, n), F32) for n in widths],
        compiler_params=_params(("parallel",), 48),
        name="in_proj",
    )(x, g, w)


def _prep_body(q_ref, k_ref, cos_ref, sa_ref, sb_ref, qg_ref, kg_ref, bd_ref, qb_ref, kf_ref, kb_ref):
    cos, sa, sb, bd = cos_ref[...], sa_ref[...], sb_ref[...], bd_ref[...]

    def norm_rope(xb, g):
        ms = jnp.dot((xb * xb).astype(BF16), bd, preferred_element_type=F32) * (1.0 / QK_DIM)
        y = xb * lax.rsqrt(ms + EPS) * g
        return y * cos + pltpu.roll(y, LANES - ROT_DIM // 2, 1) * sa + pltpu.roll(y, ROT_DIM // 2, 1) * sb

    for j in range(q_ref.shape[-1] // LANES):
        sl = slice(j * LANES, (j + 1) * LANES)
        qb_ref[:, sl] = (norm_rope(q_ref[:, sl], qg_ref[...]) * QK_SCALE).astype(BF16)
    for j in range(k_ref.shape[-1] // LANES):
        sl = slice(j * LANES, (j + 1) * LANES)
        kr = norm_rope(k_ref[:, sl], kg_ref[...])
        kf_ref[:, sl] = kr
        kb_ref[:, sl] = kr.astype(BF16)


def _qk_prep(q, k, tables, qg, kg, bd, tm, table_blocks):
    rows = q.shape[0]
    row_spec = lambda n: pl.BlockSpec((tm, n), lambda i: (i, 0))
    tab_spec = pl.BlockSpec((tm, LANES), lambda i: (i % table_blocks, 0))
    return pl.pallas_call(
        _prep_body,
        grid=(rows // tm,),
        in_specs=[row_spec(q.shape[1]), row_spec(k.shape[1]), tab_spec, tab_spec, tab_spec,
                  _resident(qg.shape), _resident(kg.shape), _resident(bd.shape)],
        out_specs=[row_spec(q.shape[1]), row_spec(k.shape[1]), row_spec(k.shape[1])],
        out_shape=[jax.ShapeDtypeStruct(q.shape, BF16), jax.ShapeDtypeStruct(k.shape, F32),
                   jax.ShapeDtypeStruct(k.shape, BF16)],
        compiler_params=_params(("parallel",), 32),
        name="qk_prep",
    )(q, k, *tables, qg, kg, bd)


def _rope_tables(pos):
    half = ROT_DIM // 2
    inv = ROPE_THETA ** (-jnp.arange(half, dtype=F32) * 2.0 / ROT_DIM)
    ang = pos.astype(F32)[:, None] * inv[None, :]
    cos, sin = jnp.cos(ang), jnp.sin(ang)
    n = pos.shape[0]
    rest = QK_DIM - ROT_DIM
    cos_h = jnp.concatenate([cos, cos, jnp.ones((n, rest), F32)], axis=1)
    sa_h = jnp.concatenate([-sin, jnp.zeros((n, half + rest), F32)], axis=1)
    sb_h = jnp.concatenate([jnp.zeros((n, half), F32), sin, jnp.zeros((n, rest), F32)], axis=1)
    rep = LANES // QK_DIM
    return tuple(jnp.tile(t, (1, rep)) for t in (cos_h, sa_h, sb_h))


def _ssd_body(z_ref, xbc_ref, dt_ref, cw_ref, cb_ref, dtb_ref, a_ref, e_ref, dsk_ref, ng_ref,
              y_ref, ctail_ref, st_ref, xpad_scr, h_scr, *, ch, d_inner):
    c = pl.program_id(1)
    nh_g = d_inner // SSM_GROUPS
    pad = SUBLANES

    @pl.when(c == 0)
    def _():
        xpad_scr[0:pad, :] = jnp.zeros((pad, xpad_scr.shape[1]), F32)
        h_scr[...] = jnp.zeros(h_scr.shape, F32)

    xbc = xbc_ref[...]
    xpad_scr[pad:pad + ch, :] = xbc
    cw = cw_ref[...]
    conv = cb_ref[...] + cw[SSM_CONV - 1:SSM_CONV] * xbc
    for kk in range(SSM_CONV - 1):
        start = pad - (SSM_CONV - 1) + kk
        conv = conv + cw[kk:kk + 1] * xpad_scr[start:start + ch, :]
    xpad_scr[0:pad, :] = xpad_scr[ch:ch + pad, :]
    act = _silu(conv)
    xs = act[:, :d_inner]
    bmb = act[:, d_inner:d_inner + LANES].astype(BF16)
    cmb = act[:, d_inner + LANES:d_inner + 2 * LANES].astype(BF16)

    dt = _softplus(dt_ref[...] + dtb_ref[...])
    da = dt * a_ref[...]
    row = lax.broadcasted_iota(jnp.int32, (ch, ch), 0)
    col = lax.broadcasted_iota(jnp.int32, (ch, ch), 1)
    causal = row >= col
    cs = jnp.dot(causal.astype(F32), da, precision=HIGHEST, preferred_element_type=F32)
    cs_t = cs.T
    e = e_ref[...]
    cs_e = jnp.dot(cs, e, precision=HIGHEST, preferred_element_type=F32)
    dt_e = jnp.dot(dt, e, precision=HIGHEST, preferred_element_type=F32)
    xdt = xs * dt_e
    last = cs_e[ch - 1:ch, :]
    xdtb = xdt.astype(BF16)
    wendb = (xdt * jnp.exp(last - cs_e)).astype(BF16)
    off_scale = jnp.exp(cs_e)
    cdec = jnp.exp(last)

    lane = lax.broadcasted_iota(jnp.int32, (ch, LANES), 1)
    lo = lane < SSM_HEAD_DIM
    zero_b = jnp.zeros((ch, LANES), BF16)
    heads_g = nh_g // SSM_HEAD_DIM
    ys = []
    for g in range(SSM_GROUPS):
        bg = bmb[:, g * SSM_STATE:(g + 1) * SSM_STATE]
        cg = cmb[:, g * SSM_STATE:(g + 1) * SSM_STATE]
        cb = lax.dot_general(cg, bg, NT_DIMS, preferred_element_type=F32)
        hg = h_scr[g]
        yoff = jnp.dot(cg, hg.astype(BF16), preferred_element_type=F32)
        s_new = lax.dot_general(bg, wendb[:, g * nh_g:(g + 1) * nh_g], TN_DIMS, preferred_element_type=F32)
        h_scr[g] = hg * cdec[:, g * nh_g:(g + 1) * nh_g] + s_new
        for pr in range(heads_g // 2):
            r0 = g * heads_g + 2 * pr
            gs = []
            for r in (r0, r0 + 1):
                seg = cs[:, r:r + 1] - cs_t[r:r + 1, :]
                gs.append(jnp.where(causal, cb * jnp.exp(seg), 0.0))
            gc = jnp.concatenate(gs, axis=1).astype(BF16)
            blk = xdtb[:, r0 * SSM_HEAD_DIM:(r0 + 2) * SSM_HEAD_DIM]
            rhs = jnp.concatenate([jnp.where(lo, blk, zero_b), jnp.where(lo, zero_b, blk)], axis=0)
            yd = jnp.dot(gc, rhs, preferred_element_type=F32)
            sl = slice(r0 * SSM_HEAD_DIM, (r0 + 2) * SSM_HEAD_DIM)
            ys.append(yd + yoff[:, 2 * pr * SSM_HEAD_DIM:(2 * pr + 2) * SSM_HEAD_DIM] * off_scale[:, sl])
    y = jnp.concatenate(ys, axis=1) + xs * dsk_ref[...]
    yg = y * _silu(z_ref[...])
    outs = [_rms(yg[:, g * nh_g:(g + 1) * nh_g]) for g in range(SSM_GROUPS)]
    y_ref[...] = (jnp.concatenate(outs, axis=1) * ng_ref[...]).astype(BF16)

    @pl.when(c == pl.num_programs(1) - 1)
    def _():
        ctail_ref[0] = xpad_scr[0:pad, :]
        st_ref[0] = h_scr[...]


def _ssd_prompt(z, xbc, dt, cw, cb, dtb, a, e, dsk, ng, batch, seq):
    ch = SSM_CHUNK
    nc = seq // ch
    d_inner = z.shape[1]
    conv_dim = xbc.shape[1]
    nh_g = d_inner // SSM_GROUPS
    row_spec = lambda n: pl.BlockSpec((ch, n), lambda b, c: (b * nc + c, 0))
    return pl.pallas_call(
        functools.partial(_ssd_body, ch=ch, d_inner=d_inner),
        grid=(batch, nc),
        in_specs=[row_spec(d_inner), row_spec(conv_dim), row_spec(LANES)]
        + [_resident(t.shape) for t in (cw, cb, dtb, a, e, dsk, ng)],
        out_specs=[row_spec(d_inner),
                   pl.BlockSpec((1, SUBLANES, conv_dim), lambda b, c: (b, 0, 0)),
                   pl.BlockSpec((1, SSM_GROUPS, SSM_STATE, nh_g), lambda b, c: (b, 0, 0, 0))],
        out_shape=[jax.ShapeDtypeStruct((batch * seq, d_inner), BF16),
                   jax.ShapeDtypeStruct((batch, SUBLANES, conv_dim), F32),
                   jax.ShapeDtypeStruct((batch, SSM_GROUPS, SSM_STATE, nh_g), F32)],
        scratch_shapes=[pltpu.VMEM((SUBLANES + ch, conv_dim), F32),
                        pltpu.VMEM((SSM_GROUPS, SSM_STATE, nh_g), F32)],
        compiler_params=_params(("parallel", "arbitrary"), 40),
        name="ssd_prompt",
    )(z, xbc, dt, cw, cb, dtb, a, e, dsk, ng)


def _lambda(lq1_ref, lk1_ref, lq2_ref, lk2_ref, lam_init):
    s1 = jnp.sum(lq1_ref[...] * lk1_ref[...], axis=-1, keepdims=True)
    s2 = jnp.sum(lq2_ref[...] * lk2_ref[...], axis=-1, keepdims=True)
    return jnp.exp(s1) - jnp.exp(s2) + lam_init


def _attn_body(lq1_ref, lk1_ref, lq2_ref, lk2_ref, sg_ref, q_ref, k_ref, v_ref, o_ref,
               qp_scr, m_scr, l_scr, acc_scr, *, tq, lam_init):
    qi = pl.program_id(2)
    q = q_ref[...]
    lane = lax.broadcasted_iota(jnp.int32, (tq, LANES), 1)
    lo = lane < QK_DIM
    zero_b = jnp.zeros((tq, LANES), BF16)
    for r in range(KV_REP):
        qr = q[:, r * LANES:(r + 1) * LANES]
        qp_scr[(2 * r) * tq:(2 * r + 1) * tq, :] = jnp.where(lo, qr, zero_b)
        qp_scr[(2 * r + 1) * tq:(2 * r + 2) * tq, :] = jnp.where(lo, zero_b, qr)
    m_scr[...] = jnp.full(m_scr.shape, -jnp.inf, F32)
    l_scr[...] = jnp.zeros(l_scr.shape, F32)
    acc_scr[...] = jnp.zeros(acc_scr.shape, F32)

    def step(j, diagonal):
        start = pl.multiple_of(j * tq, tq)
        kb = k_ref[pl.ds(start, tq), :]
        vb = v_ref[pl.ds(start, tq), :]
        s = lax.dot_general(qp_scr[...], kb, NT_DIMS, preferred_element_type=F32)
        if diagonal:
            row = lax.broadcasted_iota(jnp.int32, (tq, tq), 0)
            col = lax.broadcasted_iota(jnp.int32, (tq, tq), 1)
            keep = jnp.concatenate([col <= row] * (2 * KV_REP), axis=0)
            s = jnp.where(keep, s, -jnp.inf)
        m_prev = m_scr[...]
        m_new = jnp.maximum(m_prev, jnp.max(s, axis=-1, keepdims=True))
        alpha = jnp.exp(m_prev - m_new)
        p = jnp.exp(s - m_new)
        l_scr[...] = alpha * l_scr[...] + jnp.sum(p, axis=-1, keepdims=True)
        acc_scr[...] = alpha * acc_scr[...] + jnp.dot(p.astype(BF16), vb, preferred_element_type=F32)
        m_scr[...] = m_new

    def body(j, carry):
        step(j, False)
        return carry

    lax.fori_loop(0, qi, body, 0)
    step(qi, True)

    lam = _lambda(lq1_ref, lk1_ref, lq2_ref, lk2_ref, lam_init)
    for r in range(KV_REP):
        r1 = slice((2 * r) * tq, (2 * r + 1) * tq)
        r2 = slice((2 * r + 1) * tq, (2 * r + 2) * tq)
        o = acc_scr[r1, :] / l_scr[r1, :] - lam * (acc_scr[r2, :] / l_scr[r2, :])
        y = _rms(o) * sg_ref[...] * (1.0 - lam_init)
        o_ref[:, r * LANES:(r + 1) * LANES] = y.astype(BF16)


def _attn_prompt(lams, sg, qb, kb, vb, batch, seq, lam_init, tq):
    nq = seq // tq
    rows = 2 * KV_REP * tq
    att_w = qb.shape[1]
    return pl.pallas_call(
        functools.partial(_attn_body, tq=tq, lam_init=lam_init),
        grid=(batch, KV_HEADS, nq),
        in_specs=[_resident(t.shape) for t in lams] + [_resident(sg.shape)] + [
            pl.BlockSpec((tq, KV_REP * LANES), lambda b, h, i: (b * nq + i, h)),
            pl.BlockSpec((seq, LANES), lambda b, h, i: (b, h)),
            pl.BlockSpec((seq, LANES), lambda b, h, i: (b, h))],
        out_specs=pl.BlockSpec((tq, KV_REP * LANES), lambda b, h, i: (b * nq + i, h)),
        out_shape=jax.ShapeDtypeStruct((batch * seq, att_w), BF16),
        scratch_shapes=[pltpu.VMEM((rows, LANES), BF16), pltpu.VMEM((rows, 1), F32),
                        pltpu.VMEM((rows, 1), F32), pltpu.VMEM((rows, LANES), F32)],
        compiler_params=_params(("parallel", "parallel", "arbitrary"), 40),
        name="attn_prompt",
    )(*lams, sg, qb, kb, vb)


def _ssd_pre_body(xbc_ref, s0_ref, s1_ref, s2_ref, dt_ref, cw_ref, cb_ref, dtb_ref, a_ref, e_ref,
                  xs_ref, bm_ref, cm_ref, xdt_ref, dec_ref, *, d_inner):
    cw = cw_ref[...]
    conv = (cb_ref[...] + cw[0:1] * s0_ref[...] + cw[1:2] * s1_ref[...] + cw[2:3] * s2_ref[...]
            + cw[3:4] * xbc_ref[...])
    act = _silu(conv)
    xs = act[:, :d_inner]
    dt = _softplus(dt_ref[...] + dtb_ref[...])
    e = e_ref[...]
    dt_e = jnp.dot(dt, e, precision=HIGHEST, preferred_element_type=F32)
    dec_e = jnp.dot(jnp.exp(dt * a_ref[...]), e, precision=HIGHEST, preferred_element_type=F32)
    xs_ref[...] = xs
    bm_ref[...] = act[:, d_inner:d_inner + LANES]
    cm_ref[...] = act[:, d_inner + LANES:d_inner + 2 * LANES]
    xdt_ref[...] = xs * dt_e
    dec_ref[...] = dec_e


def _ssd_sample_pre(xbc, s0, s1, s2, dt, cw, cb, dtb, a, e):
    n, d_inner = xbc.shape[0], e.shape[1]
    wide = jax.ShapeDtypeStruct((n, d_inner), F32)
    narrow = jax.ShapeDtypeStruct((n, LANES), F32)
    return pl.pallas_call(
        functools.partial(_ssd_pre_body, d_inner=d_inner),
        out_shape=[wide, narrow, narrow, wide, wide],
        compiler_params=pltpu.CompilerParams(vmem_limit_bytes=32 * MIB),
        name="ssd_sample_pre",
    )(xbc, s0, s1, s2, dt, cw, cb, dtb, a, e)


def _ssd_step_body(h0_ref, xdt_ref, dec_ref, bm_ref, cm_ref, xs_ref, z_ref, dsk_ref, ng_ref,
                   hn_ref, y_ref, *, nh_g):
    xdt, dec, bm, cm = xdt_ref[0], dec_ref[0], bm_ref[0], cm_ref[0]
    pad7 = lambda v: jnp.concatenate([v, jnp.zeros((SUBLANES - 1, v.shape[1]), F32)], axis=0)
    ones_row = pad7(jnp.ones((1, SSM_STATE), F32))
    ys = []
    for g in range(SSM_GROUPS):
        h0 = h0_ref[0, g]
        bg = bm[:, g * SSM_STATE:(g + 1) * SSM_STATE]
        cg = cm[:, g * SSM_STATE:(g + 1) * SSM_STATE]
        xg = xdt[:, g * nh_g:(g + 1) * nh_g]
        dg = dec[:, g * nh_g:(g + 1) * nh_g]
        outer = lax.dot_general(pad7(xg), pad7(bg), TN_DIMS, precision=HIGHEST, preferred_element_type=F32)
        dcol = lax.dot_general(pad7(dg), ones_row, TN_DIMS, precision=HIGHEST, preferred_element_type=F32)
        hn_ref[0, g] = h0 * dcol + outer
        ch0 = lax.dot_general(pad7(cg), h0, NT_DIMS, precision=HIGHEST, preferred_element_type=F32)[0:1]
        ys.append(dg * ch0 + xg * jnp.sum(cg * bg, axis=-1, keepdims=True))
    y = jnp.concatenate(ys, axis=1) + xs_ref[0] * dsk_ref[...]
    yg = y * _silu(z_ref[0])
    outs = [_rms(yg[:, g * nh_g:(g + 1) * nh_g]) for g in range(SSM_GROUPS)]
    y_ref[0] = (jnp.concatenate(outs, axis=1) * ng_ref[...]).astype(BF16)


def _ssd_sample_step(h0, xdt, dec, bm, cm, xs, z, dsk, ng):
    n, _, nh_g, _ = h0.shape
    d_inner = xs.shape[-1]
    row3 = lambda w: pl.BlockSpec((1, 1, w), lambda b: (b, 0, 0))
    st_spec = pl.BlockSpec((1, SSM_GROUPS, nh_g, SSM_STATE), lambda b: (b, 0, 0, 0))
    r3 = lambda t: t.reshape(n, 1, t.shape[-1])
    return pl.pallas_call(
        functools.partial(_ssd_step_body, nh_g=nh_g),
        grid=(n,),
        in_specs=[st_spec, row3(d_inner), row3(d_inner), row3(LANES), row3(LANES), row3(d_inner), row3(d_inner),
                  _resident(dsk.shape), _resident(ng.shape)],
        out_specs=[st_spec, row3(d_inner)],
        out_shape=[jax.ShapeDtypeStruct(h0.shape, F32), jax.ShapeDtypeStruct((n, 1, d_inner), BF16)],
        compiler_params=_params(("parallel",), 32),
        name="ssd_sample_step",
    )(h0, r3(xdt), r3(dec), r3(bm), r3(cm), r3(xs), r3(z), dsk, ng)


def _attn_sample_body(pt_ref, lq1_ref, lk1_ref, lq2_ref, lk2_ref, sg_ref, q_ref, kn_ref, vn_ref, *rest,
                      pages, lam_init):
    k_refs, v_refs = rest[:pages], rest[pages:2 * pages]
    o_ref, qp_scr, m_scr, l_scr, acc_scr = rest[2 * pages:]
    j = pl.program_id(1)
    nrow = 2 * KV_REP * KV_HEADS
    kw = KV_HEADS * LANES

    @pl.when(j == 0)
    def _():
        q = q_ref[0].astype(F32)
        sub = lax.broadcasted_iota(jnp.int32, (2 * KV_HEADS, kw), 0)
        blk = lax.broadcasted_iota(jnp.int32, (2 * KV_HEADS, kw), 1) // QK_DIM
        for r in range(KV_REP):
            qr = jnp.concatenate([q[:, (KV_REP * h + r) * LANES:(KV_REP * h + r + 1) * LANES]
                                  for h in range(KV_HEADS)], axis=1)
            qp_scr[r * 2 * KV_HEADS:(r + 1) * 2 * KV_HEADS, :] = jnp.where(
                sub == blk, jnp.broadcast_to(qr, (2 * KV_HEADS, kw)), 0.0).astype(BF16)
        m_scr[...] = jnp.full(m_scr.shape, -jnp.inf, F32)
        l_scr[...] = jnp.zeros(l_scr.shape, F32)
        acc_scr[...] = jnp.zeros(acc_scr.shape, F32)

    qp = qp_scr[...]
    s = jnp.concatenate(
        [lax.dot_general(qp, k_refs[i][0].astype(BF16), NT_DIMS, preferred_element_type=F32)
         for i in range(pages)], axis=1)
    m_prev = m_scr[...]
    m_new = jnp.maximum(m_prev, jnp.max(s, axis=-1, keepdims=True))
    alpha = jnp.exp(m_prev - m_new)
    p = jnp.exp(s - m_new)
    l_scr[...] = alpha * l_scr[...] + jnp.sum(p, axis=-1, keepdims=True)
    pb = p.astype(BF16)
    pv = jnp.dot(pb[:, 0:PAGE_SIZE], v_refs[0][0].astype(BF16), preferred_element_type=F32)
    for i in range(1, pages):
        pv = pv + jnp.dot(pb[:, i * PAGE_SIZE:(i + 1) * PAGE_SIZE], v_refs[i][0].astype(BF16),
                          preferred_element_type=F32)
    acc_scr[...] = alpha * acc_scr[...] + pv
    m_scr[...] = m_new

    @pl.when(j == pl.num_programs(1) - 1)
    def _():
        s_self = jnp.sum(qp.astype(F32) * kn_ref[0].astype(BF16).astype(F32), axis=-1, keepdims=True)
        m_prev = m_scr[...]
        m_fin = jnp.maximum(m_prev, s_self)
        alpha = jnp.exp(m_prev - m_fin)
        p_self = jnp.exp(s_self - m_fin)
        l_fin = alpha * l_scr[...] + p_self
        acc = (alpha * acc_scr[...] + p_self * vn_ref[0]) / l_fin
        lam = _lambda(lq1_ref, lk1_ref, lq2_ref, lk2_ref, lam_init)
        for h in range(KV_HEADS):
            for r in range(KV_REP):
                i1 = r * 2 * KV_HEADS + 2 * h
                cols = slice(h * LANES, (h + 1) * LANES)
                o = acc[i1:i1 + 1, cols] - lam * acc[i1 + 1:i1 + 2, cols]
                y = _rms(o) * sg_ref[...] * (1.0 - lam_init)
                dst = (KV_REP * h + r) * LANES
                o_ref[0, :, dst:dst + LANES] = y.astype(BF16)


def _attn_sample(page_table, lams, sg, qb, k_new, v_new, cache_k, cache_v, lam_init):
    n, n_pages = page_table.shape
    pages = PAGES_PER_STEP
    while n_pages % pages:
        pages //= 2
    att_w = qb.shape[-1]
    kw = KV_HEADS * LANES
    nrow = 2 * KV_REP * KV_HEADS
    ck = cache_k.reshape(cache_k.shape[0], PAGE_SIZE, kw)
    cv = cache_v.reshape(cache_v.shape[0], PAGE_SIZE, kw)
    const = lambda shape: pl.BlockSpec(shape, lambda b, j, pt: (0,) * len(shape))
    row3 = lambda w: pl.BlockSpec((1, 1, w), lambda b, j, pt: (b, 0, 0))

    def page_spec(i):
        return pl.BlockSpec((1, PAGE_SIZE, kw), lambda b, j, pt: (pt[b * n_pages + j * pages + i], 0, 0))

    grid_spec = pltpu.PrefetchScalarGridSpec(
        num_scalar_prefetch=1,
        grid=(n, n_pages // pages),
        in_specs=[const(t.shape) for t in lams] + [const(sg.shape), row3(att_w), row3(kw), row3(kw)]
        + [page_spec(i) for i in range(pages)] + [page_spec(i) for i in range(pages)],
        out_specs=row3(att_w),
        scratch_shapes=[pltpu.VMEM((nrow, kw), BF16), pltpu.VMEM((nrow, 1), F32),
                        pltpu.VMEM((nrow, 1), F32), pltpu.VMEM((nrow, kw), F32)],
    )
    return pl.pallas_call(
        functools.partial(_attn_sample_body, pages=pages, lam_init=lam_init),
        grid_spec=grid_spec,
        out_shape=jax.ShapeDtypeStruct((n, 1, att_w), BF16),
        compiler_params=_params(("parallel", "arbitrary"), 40),
        name="attn_sample",
    )(page_table.reshape(-1), *lams, sg, qb.reshape(n, 1, att_w), k_new.reshape(n, 1, kw),
      v_new.reshape(n, 1, kw), *([ck] * pages), *([cv] * pages))


def _outproj_body(ys_ref, ya_ref, gs_ref, ga_ref, x_ref, wso_ref, wao_ref, wo_ref, g_ref, x1_ref, hn_ref):
    o_ssd = jnp.dot(ys_ref[...], wso_ref[...], preferred_element_type=F32)
    o_att = jnp.dot(ya_ref[...], wao_ref[...], preferred_element_type=F32)
    merged = _sigmoid(gs_ref[...]) * o_ssd + _sigmoid(ga_ref[...]) * o_att
    x1 = x_ref[...] + jnp.dot(merged.astype(BF16), wo_ref[...], preferred_element_type=F32)
    x1_ref[...] = x1
    hn_ref[...] = (_rms(x1) * g_ref[...]).astype(BF16)


def _out_proj(ys, ya, gs, ga, x, wso, wao, wo, g, tm):
    rows, d = x.shape
    row_spec = lambda n: pl.BlockSpec((tm, n), lambda i: (i, 0))
    return pl.pallas_call(
        _outproj_body,
        grid=(rows // tm,),
        in_specs=[row_spec(ys.shape[1]), row_spec(ya.shape[1]), row_spec(d), row_spec(d), row_spec(d)]
        + [_resident(t.shape) for t in (wso, wao, wo, g)],
        out_specs=[row_spec(d), row_spec(d)],
        out_shape=[jax.ShapeDtypeStruct((rows, d), F32), jax.ShapeDtypeStruct((rows, d), BF16)],
        compiler_params=_params(("parallel",), 48),
        name="out_proj",
    )(ys, ya, gs, ga, x, wso, wao, wo, g)


def _ffn_chunks(dff):
    tf = 2 * LANES
    assert dff % tf == 0
    return tf, dff // tf


def _ffn_prompt_body(hn_ref, x1_ref, wup_ref, wdn_ref, cw_ref, cb_ref, y_ref, tail_ref,
                     carry_scr, ubuf_scr, *, tm, dff):
    t = pl.program_id(1)
    pad = SUBLANES

    @pl.when(t == 0)
    def _():
        carry_scr[...] = jnp.zeros(carry_scr.shape, F32)

    hn = hn_ref[...]
    tf, nchunk = _ffn_chunks(dff)
    acc = jnp.zeros(y_ref.shape, F32)
    for c in range(nchunk):
        halves = []
        for half in range(2):
            sl = slice(half * dff + c * tf, half * dff + (c + 1) * tf)
            u = jnp.dot(hn, wup_ref[:, sl], preferred_element_type=F32)
            ubuf_scr[0:pad, :] = carry_scr[:, sl]
            ubuf_scr[pad:pad + tm, :] = u
            cw = cw_ref[:, sl]
            conv = cb_ref[:, sl] + cw[FFN_CONV - 1:FFN_CONV] * u
            for kk in range(FFN_CONV - 1):
                start = pad - (FFN_CONV - 1) + kk
                conv = conv + cw[kk:kk + 1] * ubuf_scr[start:start + tm, :]
            carry_scr[:, sl] = ubuf_scr[tm:tm + pad, :]
            halves.append(conv)
        act = (_silu(halves[1]) * halves[0]).astype(BF16)
        acc = acc + jnp.dot(act, wdn_ref[c * tf:(c + 1) * tf, :], preferred_element_type=F32)
    y_ref[...] = x1_ref[...] + acc

    @pl.when(t == pl.num_programs(1) - 1)
    def _():
        tail_ref[0] = carry_scr[...]


def _ffn_prompt(hn, x1, wup, wdn, cw, cb, batch, seq, tm):
    d = x1.shape[1]
    dff = wdn.shape[0]
    nt = seq // tm
    tf, _ = _ffn_chunks(dff)
    row_spec = lambda n: pl.BlockSpec((tm, n), lambda b, t: (b * nt + t, 0))
    return pl.pallas_call(
        functools.partial(_ffn_prompt_body, tm=tm, dff=dff),
        grid=(batch, nt),
        in_specs=[row_spec(d), row_spec(d)] + [_resident(t.shape) for t in (wup, wdn, cw, cb)],
        out_specs=[row_spec(d), pl.BlockSpec((1, SUBLANES, 2 * dff), lambda b, t: (b, 0, 0))],
        out_shape=[jax.ShapeDtypeStruct(x1.shape, F32), jax.ShapeDtypeStruct((batch, SUBLANES, 2 * dff), F32)],
        scratch_shapes=[pltpu.VMEM((SUBLANES, 2 * dff), F32), pltpu.VMEM((SUBLANES + tm, tf), F32)],
        compiler_params=_params(("parallel", "arbitrary"), 52),
        name="ffn_prompt",
    )(hn, x1, wup, wdn, cw, cb)


def _ffn_sample_body(hn_ref, x1_ref, s0_ref, s1_ref, wup_ref, wdn_ref, cw_ref, cb_ref, y_ref, u_ref, *, dff):
    hn = hn_ref[...]
    tf, nchunk = _ffn_chunks(dff)
    acc = jnp.zeros(y_ref.shape, F32)
    for c in range(nchunk):
        halves = []
        for half in range(2):
            sl = slice(half * dff + c * tf, half * dff + (c + 1) * tf)
            u = jnp.dot(hn, wup_ref[:, sl], preferred_element_type=F32)
            u_ref[:, sl] = u
            cw = cw_ref[:, sl]
            halves.append(cb_ref[:, sl] + cw[0:1] * s0_ref[:, sl] + cw[1:2] * s1_ref[:, sl] + cw[2:3] * u)
        act = (_silu(halves[1]) * halves[0]).astype(BF16)
        acc = acc + jnp.dot(act, wdn_ref[c * tf:(c + 1) * tf, :], preferred_element_type=F32)
    y_ref[...] = x1_ref[...] + acc


def _ffn_sample(hn, x1, s0, s1, wup, wdn, cw, cb):
    dff = wdn.shape[0]
    return pl.pallas_call(
        functools.partial(_ffn_sample_body, dff=dff),
        out_shape=[jax.ShapeDtypeStruct(x1.shape, F32), jax.ShapeDtypeStruct((x1.shape[0], 2 * dff), F32)],
        compiler_params=pltpu.CompilerParams(vmem_limit_bytes=52 * MIB),
        name="ffn_sample",
    )(hn, x1, s0, s1, wup, wdn, cw, cb)


def _largest_tile(n, cap):
    t = cap
    while n % t:
        t //= 2
    return t


def kernel(x_prompt, x_sample, cache_k, cache_v, state_conv, state_ssm, state_ffn_conv, page_table, norm_mix_g, w_in, ssm_conv_w, ssm_conv_b, ssm_dt_bias, ssm_a_log, ssm_d, ssm_norm_g, q_norm_g, k_norm_g, lambda_q1, lambda_k1, lambda_q2, lambda_k2, attn_subln_g, w_ssd_o, w_attn_o, w_o, norm_ffn_g, w_ffn_up, ffn_conv_w, ffn_conv_b, w_ffn_down):
    bp, sp, d = x_prompt.shape
    bs, ss, _ = x_sample.shape
    assert ss == 1, "the sample group decodes one token per sequence"
    depth = w_in.shape[0]
    past = page_table.shape[1] * PAGE_SIZE
    d_inner = w_ssd_o.shape[1]
    n_heads = ssm_dt_bias.shape[1]
    conv_dim = ssm_conv_w.shape[2]
    att_w = w_attn_o.shape[1]
    kw = KV_HEADS * 2 * QK_DIM
    vw = KV_HEADS * V_DIM
    dff = w_ffn_down.shape[1]
    heads_g = n_heads // SSM_GROUPS
    nh_g = d_inner // SSM_GROUPS

    sizes = (d_inner, conv_dim, n_heads, att_w, kw, vw, d, d)
    cuts = np.concatenate([[0], np.cumsum(sizes)])
    order = (0, 1, 3, 4, 5, 6, 7, 2)
    widths = tuple(sizes[i] for i in order[:-1]) + (LANES,)

    tables_p = _rope_tables(jnp.arange(sp))
    tables_s = _rope_tables(jnp.broadcast_to(past + jnp.arange(ss), (bs,)))
    bd = jnp.asarray(np.kron(np.eye(LANES // QK_DIM), np.ones((QK_DIM, QK_DIM))), BF16)
    expand = jnp.asarray(np.pad(np.kron(np.eye(n_heads), np.ones((1, SSM_HEAD_DIM))),
                                ((0, LANES - n_heads), (0, 0))), F32)
    pad_heads = lambda v: jnp.pad(v.astype(F32), (0, LANES - n_heads)).reshape(1, LANES)

    xp = x_prompt.reshape(bp * sp, d)
    xs_ = x_sample.reshape(bs * ss, d)
    tm_p = _largest_tile(bp * sp, 256)
    tq = _largest_tile(sp, 256)
    tm_f = _largest_tile(sp, 512)

    outs = [[] for _ in range(10)]
    for layer in range(depth):
        lam_init = 0.8 - 0.6 * math.exp(-0.3 * layer)
        wl = w_in[layer]
        w_perm = jnp.concatenate([wl[:, cuts[i]:cuts[i + 1]] for i in order]
                                 + [jnp.zeros((d, LANES - n_heads), F32)], axis=1).astype(BF16)
        g_mix = norm_mix_g[layer].reshape(1, d)
        cw, cb = ssm_conv_w[layer], ssm_conv_b[layer].reshape(1, conv_dim)
        dtb = pad_heads(ssm_dt_bias[layer])
        a_neg = pad_heads(-jnp.exp(ssm_a_log[layer].astype(F32)))
        dsk = jnp.repeat(ssm_d[layer].astype(F32), SSM_HEAD_DIM).reshape(1, d_inner)
        ng = ssm_norm_g[layer].reshape(1, d_inner)
        qg = jnp.tile(q_norm_g[layer], LANES // QK_DIM).reshape(1, LANES)
        kg = jnp.tile(k_norm_g[layer], LANES // QK_DIM).reshape(1, LANES)
        lams = tuple(t[layer].reshape(1, QK_DIM) for t in (lambda_q1, lambda_k1, lambda_q2, lambda_k2))
        sg = attn_subln_g[layer].reshape(1, V_DIM)
        wso, wao, wo = (t[layer].astype(BF16) for t in (w_ssd_o, w_attn_o, w_o))
        g_ffn = norm_ffn_g[layer].reshape(1, d)
        wup, wdn = w_ffn_up[layer].astype(BF16), w_ffn_down[layer].astype(BF16)
        fcw, fcb = ffn_conv_w[layer], ffn_conv_b[layer].reshape(1, 2 * dff)

        z, xbc, q, k, v, gs, ga, dt = _in_proj(xp, g_mix, w_perm, widths, tm_p)
        y_ssd, ctail, st = _ssd_prompt(z, xbc, dt, cw, cb, dtb, a_neg, expand, dsk, ng, bp, sp)
        qb, kf, kb = _qk_prep(q, k, tables_p, qg, kg, bd, tm_p, sp // tm_p)
        y_att = _attn_prompt(lams, sg, qb, kb, v.astype(BF16), bp, sp, lam_init, tq)
        x1, hn = _out_proj(y_ssd, y_att, gs, ga, xp, wso, wao, wo, g_ffn, tm_p)
        xp, ftail = _ffn_prompt(hn, x1, wup, wdn, fcw, fcb, bp, sp, tm_f)
        outs[0].append(kf.reshape(bp, sp, KV_HEADS, 2 * QK_DIM))
        outs[1].append(v.reshape(bp, sp, KV_HEADS, V_DIM))
        outs[2].append(ctail[:, SUBLANES - (SSM_CONV - 1):, :])
        outs[3].append(st.reshape(bp, SSM_GROUPS, SSM_STATE, heads_g, SSM_HEAD_DIM).transpose(0, 1, 3, 4, 2))
        outs[4].append(ftail[:, SUBLANES - (FFN_CONV - 1):, :])

        z, xbc, q, k, v, gs, ga, dt = _in_proj(xs_, g_mix, w_perm, widths, bs)
        sc = state_conv[layer]
        xs_c, bm, cm, xdt, dec = _ssd_sample_pre(xbc, sc[:, 0], sc[:, 1], sc[:, 2], dt, cw, cb, dtb, a_neg, expand)
        h0 = state_ssm[layer].reshape(bs, SSM_GROUPS, nh_g, SSM_STATE)
        h_new, y_ssd = _ssd_sample_step(h0, xdt, dec, bm, cm, xs_c, z, dsk, ng)
        qb, kf, _ = _qk_prep(q, k, tables_s, qg, kg, bd, bs, 1)
        y_att = _attn_sample(page_table, lams, sg, qb, kf, v, cache_k[layer], cache_v[layer], lam_init)
        x1, hn = _out_proj(y_ssd.reshape(bs, d_inner), y_att.reshape(bs, att_w), gs, ga, xs_, wso, wao, wo, g_ffn, bs)
        sf = state_ffn_conv[layer]
        xs_, u_new = _ffn_sample(hn, x1, sf[:, 0], sf[:, 1], wup, wdn, fcw, fcb)
        outs[5].append(kf.reshape(bs, ss, KV_HEADS, 2 * QK_DIM))
        outs[6].append(v.reshape(bs, ss, KV_HEADS, V_DIM))
        outs[7].append(jnp.concatenate([sc[:, 1:], xbc[:, None, :]], axis=1))
        outs[8].append(h_new.reshape(state_ssm.shape[1:]))
        outs[9].append(jnp.concatenate([sf[:, 1:], u_new[:, None, :]], axis=1))

    stacked = [jnp.stack(o) for o in outs]
    return (xp.reshape(bp, sp, d), xs_.reshape(bs, ss, d), *stacked)
```

```python
import functools
import math

import jax
import jax.numpy as jnp
import numpy as np
from jax import lax
from jax.experimental import pallas as pl
from jax.experimental.pallas import tpu as pltpu

F32 = jnp.float32
BF16 = jnp.bfloat16
HIGHEST = lax.Precision.HIGHEST

EPS = 1e-6
PAGE_SIZE = 128
SSM_HEAD_DIM = 64
SSM_GROUPS = 2
SSM_STATE = 64
SSM_CONV = 4
SSM_CHUNK = 128
KV_HEADS = 4
KV_REP = 2
QK_DIM = 64
V_DIM = 128
ROT_DIM = QK_DIM // 4
ROPE_THETA = 500000.0
FFN_CONV = 3
QK_SCALE = QK_DIM ** -0.5 * math.log2(math.e)
ONES_ROWS = 16
LANES = 128
SUBLANES = 8
PAGES_PER_STEP = 8
MIB = 1024 * 1024

NT_DIMS = (((1,), (1,)), ((), ()))
TN_DIMS = (((0,), (0,)), ((), ()))


def _params(semantics, vmem_mib):
    return pltpu.CompilerParams(dimension_semantics=semantics, vmem_limit_bytes=vmem_mib * MIB)


def _resident(shape):
    nd = len(shape)
    return pl.BlockSpec(shape, lambda *_: (0,) * nd, pipeline_mode=pl.Buffered(1))


def _sigmoid(x):
    return 1.0 / (1.0 + jnp.exp(-x))


def _silu(x):
    return x * _sigmoid(x)


def _softplus(x):
    return jnp.maximum(x, 0.0) + jnp.log1p(jnp.exp(-jnp.abs(x)))


def _rms(x, axis=-1):
    return x * lax.rsqrt(jnp.mean(x * x, axis=axis, keepdims=True) + EPS)


def _inproj_body(x_ref, g_ref, w_ref, cos_ref, sa_ref, sb_ref, qg_ref, kg_ref, bd_ref,
                 z_ref, xbc_ref, qb_ref, kf_ref, kb_ref, v_ref, vt_ref, gs_ref, ga_ref, dt_ref):
    h = (_rms(x_ref[...]) * g_ref[...]).astype(BF16)
    cos, sa, sb, bd = cos_ref[...], sa_ref[...], sb_ref[...], bd_ref[...]
    off = [0]

    def proj(n):
        u = jnp.dot(h, w_ref[:, off[0]:off[0] + n], preferred_element_type=F32)
        off[0] += n
        return u

    def norm_rope(xb, g):
        ms = jnp.dot((xb * xb).astype(BF16), bd, preferred_element_type=F32) * (1.0 / QK_DIM)
        y = xb * lax.rsqrt(ms + EPS) * g
        return y * cos + pltpu.roll(y, LANES - ROT_DIM // 2, 1) * sa + pltpu.roll(y, ROT_DIM // 2, 1) * sb

    z_ref[...] = proj(z_ref.shape[-1])
    xbc_ref[...] = proj(xbc_ref.shape[-1])
    q = proj(qb_ref.shape[-1])
    for j in range(q.shape[-1] // LANES):
        sl = slice(j * LANES, (j + 1) * LANES)
        qb_ref[:, sl] = (norm_rope(q[:, sl], qg_ref[...]) * QK_SCALE).astype(BF16)
    k = proj(kf_ref.shape[-1])
    for j in range(k.shape[-1] // LANES):
        sl = slice(j * LANES, (j + 1) * LANES)
        kr = norm_rope(k[:, sl], kg_ref[...])
        kf_ref[:, sl] = kr
        kb_ref[:, sl] = kr.astype(BF16)
    v = proj(v_ref.shape[-1])
    v_ref[...] = v
    ones = jnp.ones((ONES_ROWS, v.shape[0]), BF16)
    for hd in range(KV_HEADS):
        vt_ref[0, hd, :V_DIM, :] = v[:, hd * V_DIM:(hd + 1) * V_DIM].T.astype(BF16)
        vt_ref[0, hd, V_DIM:, :] = ones
    gs_ref[...] = proj(gs_ref.shape[-1])
    ga_ref[...] = proj(ga_ref.shape[-1])
    dt_ref[...] = proj(dt_ref.shape[-1])


def _in_proj(x, g, w, tables, qg, kg, bd, widths, tm, seq):
    rows, d = x.shape
    nt = seq // tm
    wz, wxbc, wq, wk, wv, wgs, wga, wdt = widths
    row = lambda n: pl.BlockSpec((tm, n), lambda i: (i, 0))
    tab = pl.BlockSpec((tm, LANES), lambda i: (i % nt, 0))
    f32 = lambda n: jax.ShapeDtypeStruct((rows, n), F32)
    b16 = lambda n: jax.ShapeDtypeStruct((rows, n), BF16)
    vt_rows = V_DIM + ONES_ROWS
    return pl.pallas_call(
        _inproj_body,
        grid=(rows // tm,),
        in_specs=[row(d), _resident(g.shape), _resident(w.shape), tab, tab, tab,
                  _resident(qg.shape), _resident(kg.shape), _resident(bd.shape)],
        out_specs=[row(wz), row(wxbc), row(wq), row(wk), row(wk), row(wv),
                   pl.BlockSpec((1, KV_HEADS, vt_rows, tm), lambda i: (i // nt, 0, 0, i % nt)),
                   row(wgs), row(wga), row(wdt)],
        out_shape=[f32(wz), f32(wxbc), b16(wq), f32(wk), b16(wk), f32(wv),
                   jax.ShapeDtypeStruct((rows // seq, KV_HEADS, vt_rows, seq), BF16),
                   f32(wgs), f32(wga), f32(wdt)],
        compiler_params=_params(("parallel",), 52),
        name="in_proj",
    )(x, g, w, *tables, qg, kg, bd)


def _rope_tables(pos):
    half = ROT_DIM // 2
    inv = ROPE_THETA ** (-jnp.arange(half, dtype=F32) * 2.0 / ROT_DIM)
    ang = pos.astype(F32)[:, None] * inv[None, :]
    cos, sin = jnp.cos(ang), jnp.sin(ang)
    n = pos.shape[0]
    rest = QK_DIM - ROT_DIM
    cos_h = jnp.concatenate([cos, cos, jnp.ones((n, rest), F32)], axis=1)
    sa_h = jnp.concatenate([-sin, jnp.zeros((n, half + rest), F32)], axis=1)
    sb_h = jnp.concatenate([jnp.zeros((n, half), F32), sin, jnp.zeros((n, rest), F32)], axis=1)
    rep = LANES // QK_DIM
    return tuple(jnp.tile(t, (1, rep)) for t in (cos_h, sa_h, sb_h))


def _ssd_body(z_ref, xbc_ref, dt_ref, cw_ref, cb_ref, dtb_ref, a_ref, e_ref, dsk_ref, ng_ref,
              y_ref, ctail_ref, st_ref, xpad_scr, h_scr, *, ch, d_inner):
    c = pl.program_id(1)
    nh_g = d_inner // SSM_GROUPS
    pad = SUBLANES

    @pl.when(c == 0)
    def _():
        xpad_scr[0:pad, :] = jnp.zeros((pad, xpad_scr.shape[1]), F32)
        h_scr[...] = jnp.zeros(h_scr.shape, F32)

    xbc = xbc_ref[...]
    xpad_scr[pad:pad + ch, :] = xbc
    cw = cw_ref[...]
    conv = cb_ref[...] + cw[SSM_CONV - 1:SSM_CONV] * xbc
    for kk in range(SSM_CONV - 1):
        start = pad - (SSM_CONV - 1) + kk
        conv = conv + cw[kk:kk + 1] * xpad_scr[start:start + ch, :]
    xpad_scr[0:pad, :] = xpad_scr[ch:ch + pad, :]
    act = _silu(conv)
    xs = act[:, :d_inner]
    bmb = act[:, d_inner:d_inner + LANES].astype(BF16)
    cmb = act[:, d_inner + LANES:d_inner + 2 * LANES].astype(BF16)

    dt = _softplus(dt_ref[...] + dtb_ref[...])
    da = dt * a_ref[...]
    row = lax.broadcasted_iota(jnp.int32, (ch, ch), 0)
    col = lax.broadcasted_iota(jnp.int32, (ch, ch), 1)
    causal = row >= col
    cs = jnp.dot(causal.astype(F32), da, precision=HIGHEST, preferred_element_type=F32)
    cs_t = cs.T
    e = e_ref[...]
    cs_e = jnp.dot(cs, e, precision=HIGHEST, preferred_element_type=F32)
    dt_e = jnp.dot(dt, e, precision=HIGHEST, preferred_element_type=F32)
    xdt = xs * dt_e
    last = cs_e[ch - 1:ch, :]
    xdtb = xdt.astype(BF16)
    wendb = (xdt * jnp.exp(last - cs_e)).astype(BF16)
    off_scale = jnp.exp(cs_e)
    cdec = jnp.exp(last)

    lane = lax.broadcasted_iota(jnp.int32, (ch, LANES), 1)
    lo = lane < SSM_HEAD_DIM
    zero_b = jnp.zeros((ch, LANES), BF16)
    heads_g = nh_g // SSM_HEAD_DIM
    ys = []
    for g in range(SSM_GROUPS):
        bg = bmb[:, g * SSM_STATE:(g + 1) * SSM_STATE]
        cg = cmb[:, g * SSM_STATE:(g + 1) * SSM_STATE]
        cb = lax.dot_general(cg, bg, NT_DIMS, preferred_element_type=F32)
        hg = h_scr[g]
        yoff = jnp.dot(cg, hg.astype(BF16), preferred_element_type=F32)
        s_new = lax.dot_general(bg, wendb[:, g * nh_g:(g + 1) * nh_g], TN_DIMS, preferred_element_type=F32)
        h_scr[g] = hg * cdec[:, g * nh_g:(g + 1) * nh_g] + s_new
        for pr in range(heads_g // 2):
            r0 = g * heads_g + 2 * pr
            gs = []
            for r in (r0, r0 + 1):
                seg = cs[:, r:r + 1] - cs_t[r:r + 1, :]
                gs.append(jnp.where(causal, cb * jnp.exp(seg), 0.0))
            gc = jnp.concatenate(gs, axis=1).astype(BF16)
            blk = xdtb[:, r0 * SSM_HEAD_DIM:(r0 + 2) * SSM_HEAD_DIM]
            rhs = jnp.concatenate([jnp.where(lo, blk, zero_b), jnp.where(lo, zero_b, blk)], axis=0)
            yd = jnp.dot(gc, rhs, preferred_element_type=F32)
            sl = slice(r0 * SSM_HEAD_DIM, (r0 + 2) * SSM_HEAD_DIM)
            ys.append(yd + yoff[:, 2 * pr * SSM_HEAD_DIM:(2 * pr + 2) * SSM_HEAD_DIM] * off_scale[:, sl])
    y = jnp.concatenate(ys, axis=1) + xs * dsk_ref[...]
    yg = y * _silu(z_ref[...])
    outs = [_rms(yg[:, g * nh_g:(g + 1) * nh_g]) for g in range(SSM_GROUPS)]
    y_ref[...] = (jnp.concatenate(outs, axis=1) * ng_ref[...]).astype(BF16)

    @pl.when(c == pl.num_programs(1) - 1)
    def _():
        ctail_ref[0] = xpad_scr[0:pad, :]
        st_ref[0] = h_scr[...]


def _ssd_prompt(z, xbc, dt, cw, cb, dtb, a, e, dsk, ng, batch, seq):
    ch = SSM_CHUNK
    nc = seq // ch
    d_inner = z.shape[1]
    conv_dim = xbc.shape[1]
    nh_g = d_inner // SSM_GROUPS
    row_spec = lambda n: pl.BlockSpec((ch, n), lambda b, c: (b * nc + c, 0))
    return pl.pallas_call(
        functools.partial(_ssd_body, ch=ch, d_inner=d_inner),
        grid=(batch, nc),
        in_specs=[row_spec(d_inner), row_spec(conv_dim), row_spec(LANES)]
        + [_resident(t.shape) for t in (cw, cb, dtb, a, e, dsk, ng)],
        out_specs=[row_spec(d_inner),
                   pl.BlockSpec((1, SUBLANES, conv_dim), lambda b, c: (b, 0, 0)),
                   pl.BlockSpec((1, SSM_GROUPS, SSM_STATE, nh_g), lambda b, c: (b, 0, 0, 0))],
        out_shape=[jax.ShapeDtypeStruct((batch * seq, d_inner), BF16),
                   jax.ShapeDtypeStruct((batch, SUBLANES, conv_dim), F32),
                   jax.ShapeDtypeStruct((batch, SSM_GROUPS, SSM_STATE, nh_g), F32)],
        scratch_shapes=[pltpu.VMEM((SUBLANES + ch, conv_dim), F32),
                        pltpu.VMEM((SSM_GROUPS, SSM_STATE, nh_g), F32)],
        compiler_params=_params(("parallel", "arbitrary"), 40),
        name="ssd_prompt",
    )(z, xbc, dt, cw, cb, dtb, a, e, dsk, ng)


def _lambda(lq1_ref, lk1_ref, lq2_ref, lk2_ref, lam_init):
    s1 = jnp.sum(lq1_ref[...] * lk1_ref[...], axis=-1, keepdims=True)
    s2 = jnp.sum(lq2_ref[...] * lk2_ref[...], axis=-1, keepdims=True)
    return jnp.exp(s1) - jnp.exp(s2) + lam_init


def _attn_body(lq1_ref, lk1_ref, lq2_ref, lk2_ref, sg_ref, q_ref, k_ref, vt_ref, o_ref,
               qp_scr, m_scr, acc_scr, sa_scr, sb_scr, *, tq, lam_init):
    qi = pl.program_id(2)
    q = q_ref[...]
    lane = lax.broadcasted_iota(jnp.int32, (tq, LANES), 1)
    lo = lane < QK_DIM
    zero_b = jnp.zeros((tq, LANES), BF16)
    for r in range(KV_REP):
        qr = q[:, r * LANES:(r + 1) * LANES]
        qp_scr[(2 * r) * tq:(2 * r + 1) * tq, :] = jnp.where(lo, qr, zero_b)
        qp_scr[(2 * r + 1) * tq:(2 * r + 2) * tq, :] = jnp.where(lo, zero_b, qr)
    m_scr[...] = jnp.full(m_scr.shape, -jnp.inf, F32)
    acc_scr[...] = jnp.zeros(acc_scr.shape, F32)

    def scores(j, s_scr):
        start = pl.multiple_of(j * tq, tq)
        s_scr[...] = lax.dot_general(k_ref[pl.ds(start, tq), :], qp_scr[...], NT_DIMS,
                                     preferred_element_type=F32)

    def consume(j, s_scr, diagonal):
        start = pl.multiple_of(j * tq, tq)
        vtb = vt_ref[:, pl.ds(start, tq)]
        m_prev = m_scr[...]
        ps, ms = [], []
        for c in range(2 * KV_REP * tq // LANES):
            cols = slice(c * LANES, (c + 1) * LANES)
            s = s_scr[:, cols]
            if diagonal:
                key = lax.broadcasted_iota(jnp.int32, (tq, LANES), 0)
                qry = lax.broadcasted_iota(jnp.int32, (tq, LANES), 1) + (c * LANES) % tq
                s = jnp.where(key <= qry, s, -jnp.inf)
            m_c = jnp.maximum(m_prev[:, cols], jnp.max(s, axis=0, keepdims=True))
            ps.append(jnp.exp2(s - m_c).astype(BF16))
            ms.append(m_c)
        m_new = jnp.concatenate(ms, axis=1)
        alpha = jnp.exp2(m_prev - m_new)
        pv = jnp.dot(vtb, jnp.concatenate(ps, axis=1), preferred_element_type=F32)
        acc_scr[...] = alpha * acc_scr[...] + pv
        m_scr[...] = m_new

    def pair(i, carry):
        scores(2 * i + 1, sb_scr)
        consume(2 * i, sa_scr, False)
        scores(2 * i + 2, sa_scr)
        consume(2 * i + 1, sb_scr, False)
        return carry

    scores(0, sa_scr)
    lax.fori_loop(0, qi // 2, pair, 0)

    @pl.when(qi % 2 == 1)
    def _():
        scores(qi, sb_scr)
        consume(qi - 1, sa_scr, False)
        consume(qi, sb_scr, True)

    @pl.when(qi % 2 == 0)
    def _():
        consume(qi, sa_scr, True)

    lam = _lambda(lq1_ref, lk1_ref, lq2_ref, lk2_ref, lam_init)
    for r in range(KV_REP):
        c1 = slice((2 * r) * tq, (2 * r + 1) * tq)
        c2 = slice((2 * r + 1) * tq, (2 * r + 2) * tq)
        o = (acc_scr[:V_DIM, c1] / acc_scr[V_DIM:V_DIM + 1, c1]
             - lam * (acc_scr[:V_DIM, c2] / acc_scr[V_DIM:V_DIM + 1, c2]))
        y = _rms(o, axis=0) * sg_ref[...] * (1.0 - lam_init)
        o_ref[:, r * LANES:(r + 1) * LANES] = y.T.astype(BF16)


def _attn_prompt(lams, sg_col, qb, kb, vt, batch, seq, lam_init, tq):
    nq = seq // tq
    cols = 2 * KV_REP * tq
    att_w = qb.shape[1]
    return pl.pallas_call(
        functools.partial(_attn_body, tq=tq, lam_init=lam_init),
        grid=(batch, KV_HEADS, nq),
        in_specs=[_resident(t.shape) for t in lams] + [_resident(sg_col.shape)] + [
            pl.BlockSpec((tq, KV_REP * LANES), lambda b, h, i: (b * nq + i, h)),
            pl.BlockSpec((seq, LANES), lambda b, h, i: (b, h)),
            pl.BlockSpec((V_DIM + ONES_ROWS, seq), lambda b, h, i: (b * KV_HEADS + h, 0))],
        out_specs=pl.BlockSpec((tq, KV_REP * LANES), lambda b, h, i: (b * nq + i, h)),
        out_shape=jax.ShapeDtypeStruct((batch * seq, att_w), BF16),
        scratch_shapes=[pltpu.VMEM((cols, LANES), BF16), pltpu.VMEM((1, cols), F32),
                        pltpu.VMEM((V_DIM + ONES_ROWS, cols), F32), pltpu.VMEM((tq, cols), F32),
                        pltpu.VMEM((tq, cols), F32)],
        compiler_params=_params(("parallel", "parallel", "arbitrary"), 40),
        name="attn_prompt",
    )(*lams, sg_col, qb, kb, vt)


def _ssd_pre_body(xbc_ref, s0_ref, s1_ref, s2_ref, dt_ref, cw_ref, cb_ref, dtb_ref, a_ref, e_ref,
                  xs_ref, bm_ref, cm_ref, xdt_ref, dec_ref, *, d_inner):
    cw = cw_ref[...]
    conv = (cb_ref[...] + cw[0:1] * s0_ref[...] + cw[1:2] * s1_ref[...] + cw[2:3] * s2_ref[...]
            + cw[3:4] * xbc_ref[...])
    act = _silu(conv)
    xs = act[:, :d_inner]
    dt = _softplus(dt_ref[...] + dtb_ref[...])
    e = e_ref[...]
    dt_e = jnp.dot(dt, e, precision=HIGHEST, preferred_element_type=F32)
    dec_e = jnp.dot(jnp.exp(dt * a_ref[...]), e, precision=HIGHEST, preferred_element_type=F32)
    xs_ref[...] = xs
    bm_ref[...] = act[:, d_inner:d_inner + LANES]
    cm_ref[...] = act[:, d_inner + LANES:d_inner + 2 * LANES]
    xdt_ref[...] = xs * dt_e
    dec_ref[...] = dec_e


def _ssd_sample_pre(xbc, s0, s1, s2, dt, cw, cb, dtb, a, e):
    n, d_inner = xbc.shape[0], e.shape[1]
    wide = jax.ShapeDtypeStruct((n, d_inner), F32)
    narrow = jax.ShapeDtypeStruct((n, LANES), F32)
    return pl.pallas_call(
        functools.partial(_ssd_pre_body, d_inner=d_inner),
        out_shape=[wide, narrow, narrow, wide, wide],
        compiler_params=pltpu.CompilerParams(vmem_limit_bytes=32 * MIB),
        name="ssd_sample_pre",
    )(xbc, s0, s1, s2, dt, cw, cb, dtb, a, e)


def _ssd_step_body(h0_ref, xdt_ref, dec_ref, bm_ref, cm_ref, xs_ref, z_ref, dsk_ref, ng_ref,
                   hn_ref, y_ref, *, nh_g):
    xdt, dec, bm, cm = xdt_ref[0], dec_ref[0], bm_ref[0], cm_ref[0]
    pad7 = lambda v: jnp.concatenate([v, jnp.zeros((SUBLANES - 1, v.shape[1]), F32)], axis=0)
    ones_row = pad7(jnp.ones((1, SSM_STATE), F32))
    ys = []
    for g in range(SSM_GROUPS):
        h0 = h0_ref[0, 0, g].reshape(nh_g, SSM_STATE)
        bg = bm[:, g * SSM_STATE:(g + 1) * SSM_STATE]
        cg = cm[:, g * SSM_STATE:(g + 1) * SSM_STATE]
        xg = xdt[:, g * nh_g:(g + 1) * nh_g]
        dg = dec[:, g * nh_g:(g + 1) * nh_g]
        outer = lax.dot_general(pad7(xg), pad7(bg), TN_DIMS, precision=HIGHEST, preferred_element_type=F32)
        dcol = lax.dot_general(pad7(dg), ones_row, TN_DIMS, precision=HIGHEST, preferred_element_type=F32)
        hn_ref[0, g] = (h0 * dcol + outer).reshape(hn_ref.shape[2:])
        ch0 = lax.dot_general(pad7(cg), h0, NT_DIMS, precision=HIGHEST, preferred_element_type=F32)[0:1]
        ys.append(dg * ch0 + xg * jnp.sum(cg * bg, axis=-1, keepdims=True))
    y = jnp.concatenate(ys, axis=1) + xs_ref[0] * dsk_ref[...]
    yg = y * _silu(z_ref[0])
    outs = [_rms(yg[:, g * nh_g:(g + 1) * nh_g]) for g in range(SSM_GROUPS)]
    y_ref[0] = (jnp.concatenate(outs, axis=1) * ng_ref[...]).astype(BF16)


def _ssd_sample_step(state, layer, xdt, dec, bm, cm, xs, z, dsk, ng):
    n = state.shape[1]
    st_dims = state.shape[2:]
    d_inner = xs.shape[-1]
    row3 = lambda w: pl.BlockSpec((1, 1, w), lambda b: (b, 0, 0))
    r3 = lambda t: t.reshape(n, 1, t.shape[-1])
    return pl.pallas_call(
        functools.partial(_ssd_step_body, nh_g=d_inner // SSM_GROUPS),
        grid=(n,),
        in_specs=[pl.BlockSpec((1, 1) + st_dims, lambda b: (layer, b, 0, 0, 0, 0)),
                  row3(d_inner), row3(d_inner), row3(LANES), row3(LANES), row3(d_inner), row3(d_inner),
                  _resident(dsk.shape), _resident(ng.shape)],
        out_specs=[pl.BlockSpec((1,) + st_dims, lambda b: (b, 0, 0, 0, 0)), row3(d_inner)],
        out_shape=[jax.ShapeDtypeStruct((n,) + st_dims, F32), jax.ShapeDtypeStruct((n, 1, d_inner), BF16)],
        compiler_params=_params(("parallel",), 32),
        name="ssd_sample_step",
    )(state, r3(xdt), r3(dec), r3(bm), r3(cm), r3(xs), r3(z), dsk, ng)


def _attn_sample_body(pt_ref, lq1_ref, lk1_ref, lq2_ref, lk2_ref, sg_ref, q_ref, kn_ref, vn_ref, *rest,
                      pages, lam_init):
    k_refs, v_refs = rest[:pages], rest[pages:2 * pages]
    o_ref, qp_scr, m_scr, l_scr, acc_scr = rest[2 * pages:]
    j = pl.program_id(1)
    nrow = 2 * KV_REP * KV_HEADS
    kw = KV_HEADS * LANES

    @pl.when(j == 0)
    def _():
        q = q_ref[0].astype(F32)
        sub = lax.broadcasted_iota(jnp.int32, (2 * KV_HEADS, kw), 0)
        blk = lax.broadcasted_iota(jnp.int32, (2 * KV_HEADS, kw), 1) // QK_DIM
        for r in range(KV_REP):
            qr = jnp.concatenate([q[:, (KV_REP * h + r) * LANES:(KV_REP * h + r + 1) * LANES]
                                  for h in range(KV_HEADS)], axis=1)
            qp_scr[r * 2 * KV_HEADS:(r + 1) * 2 * KV_HEADS, :] = jnp.where(
                sub == blk, jnp.broadcast_to(qr, (2 * KV_HEADS, kw)), 0.0).astype(BF16)
        m_scr[...] = jnp.full(m_scr.shape, -jnp.inf, F32)
        l_scr[...] = jnp.zeros(l_scr.shape, F32)
        acc_scr[...] = jnp.zeros(acc_scr.shape, F32)

    qp = qp_scr[...]
    s = jnp.concatenate(
        [lax.dot_general(qp, k_refs[i][0].astype(BF16), NT_DIMS, preferred_element_type=F32)
         for i in range(pages)], axis=1)
    m_prev = m_scr[...]
    m_new = jnp.maximum(m_prev, jnp.max(s, axis=-1, keepdims=True))
    alpha = jnp.exp2(m_prev - m_new)
    p = jnp.exp2(s - m_new)
    l_scr[...] = alpha * l_scr[...] + jnp.sum(p, axis=-1, keepdims=True)
    pb = p.astype(BF16)
    pv = jnp.dot(pb[:, 0:PAGE_SIZE], v_refs[0][0].astype(BF16), preferred_element_type=F32)
    for i in range(1, pages):
        pv = pv + jnp.dot(pb[:, i * PAGE_SIZE:(i + 1) * PAGE_SIZE], v_refs[i][0].astype(BF16),
                          preferred_element_type=F32)
    acc_scr[...] = alpha * acc_scr[...] + pv
    m_scr[...] = m_new

    @pl.when(j == pl.num_programs(1) - 1)
    def _():
        s_self = jnp.sum(qp.astype(F32) * kn_ref[0].astype(BF16).astype(F32), axis=-1, keepdims=True)
        m_prev = m_scr[...]
        m_fin = jnp.maximum(m_prev, s_self)
        alpha = jnp.exp2(m_prev - m_fin)
        p_self = jnp.exp2(s_self - m_fin)
        l_fin = alpha * l_scr[...] + p_self
        acc = (alpha * acc_scr[...] + p_self * vn_ref[0]) / l_fin
        lam = _lambda(lq1_ref, lk1_ref, lq2_ref, lk2_ref, lam_init)
        for h in range(KV_HEADS):
            for r in range(KV_REP):
                i1 = r * 2 * KV_HEADS + 2 * h
                cols = slice(h * LANES, (h + 1) * LANES)
                o = acc[i1:i1 + 1, cols] - lam * acc[i1 + 1:i1 + 2, cols]
                y = _rms(o) * sg_ref[...] * (1.0 - lam_init)
                dst = (KV_REP * h + r) * LANES
                o_ref[0, :, dst:dst + LANES] = y.astype(BF16)


def _attn_sample(page_table, lams, sg, qb, k_new, v_new, cache_k, cache_v, layer, lam_init):
    n, n_pages = page_table.shape
    pages = PAGES_PER_STEP
    while n_pages % pages:
        pages //= 2
    att_w = qb.shape[-1]
    kw = KV_HEADS * LANES
    nrow = 2 * KV_REP * KV_HEADS
    n_pool = cache_k.shape[1]
    ck = cache_k.reshape(cache_k.shape[0] * n_pool, PAGE_SIZE, kw)
    cv = cache_v.reshape(cache_v.shape[0] * n_pool, PAGE_SIZE, kw)
    page_table = page_table + layer * n_pool
    const = lambda shape: pl.BlockSpec(shape, lambda b, j, pt: (0,) * len(shape))
    row3 = lambda w: pl.BlockSpec((1, 1, w), lambda b, j, pt: (b, 0, 0))

    def page_spec(i):
        return pl.BlockSpec((1, PAGE_SIZE, kw), lambda b, j, pt: (pt[b * n_pages + j * pages + i], 0, 0))

    grid_spec = pltpu.PrefetchScalarGridSpec(
        num_scalar_prefetch=1,
        grid=(n, n_pages // pages),
        in_specs=[const(t.shape) for t in lams] + [const(sg.shape), row3(att_w), row3(kw), row3(kw)]
        + [page_spec(i) for i in range(pages)] + [page_spec(i) for i in range(pages)],
        out_specs=row3(att_w),
        scratch_shapes=[pltpu.VMEM((nrow, kw), BF16), pltpu.VMEM((nrow, 1), F32),
                        pltpu.VMEM((nrow, 1), F32), pltpu.VMEM((nrow, kw), F32)],
    )
    return pl.pallas_call(
        functools.partial(_attn_sample_body, pages=pages, lam_init=lam_init),
        grid_spec=grid_spec,
        out_shape=jax.ShapeDtypeStruct((n, 1, att_w), BF16),
        compiler_params=_params(("parallel", "arbitrary"), 40),
        name="attn_sample",
    )(page_table.reshape(-1), *lams, sg, qb.reshape(n, 1, att_w), k_new.reshape(n, 1, kw),
      v_new.reshape(n, 1, kw), *([ck] * pages), *([cv] * pages))


def _outproj_body(ys_ref, ya_ref, gs_ref, ga_ref, x_ref, wso_ref, wao_ref, wo_ref, g_ref, x1_ref, hn_ref):
    o_ssd = jnp.dot(ys_ref[...], wso_ref[...], preferred_element_type=F32)
    o_att = jnp.dot(ya_ref[...], wao_ref[...], preferred_element_type=F32)
    merged = _sigmoid(gs_ref[...]) * o_ssd + _sigmoid(ga_ref[...]) * o_att
    x1 = x_ref[...] + jnp.dot(merged.astype(BF16), wo_ref[...], preferred_element_type=F32)
    x1_ref[...] = x1
    hn_ref[...] = (_rms(x1) * g_ref[...]).astype(BF16)


def _out_proj(ys, ya, gs, ga, x, wso, wao, wo, g, tm):
    rows, d = x.shape
    row_spec = lambda n: pl.BlockSpec((tm, n), lambda i: (i, 0))
    return pl.pallas_call(
        _outproj_body,
        grid=(rows // tm,),
        in_specs=[row_spec(ys.shape[1]), row_spec(ya.shape[1]), row_spec(d), row_spec(d), row_spec(d)]
        + [_resident(t.shape) for t in (wso, wao, wo, g)],
        out_specs=[row_spec(d), row_spec(d)],
        out_shape=[jax.ShapeDtypeStruct((rows, d), F32), jax.ShapeDtypeStruct((rows, d), BF16)],
        compiler_params=_params(("parallel",), 48),
        name="out_proj",
    )(ys, ya, gs, ga, x, wso, wao, wo, g)


def _ffn_chunks(dff):
    tf = 2 * LANES
    assert dff % tf == 0
    return tf, dff // tf


def _ffn_prompt_body(hn_ref, x1_ref, wup_ref, wdn_ref, cw_ref, cb_ref, y_ref, tail_ref,
                     carry_scr, ubuf_scr, *, tm, dff):
    t = pl.program_id(1)
    pad = SUBLANES

    @pl.when(t == 0)
    def _():
        carry_scr[...] = jnp.zeros(carry_scr.shape, F32)

    hn = hn_ref[...]
    tf, nchunk = _ffn_chunks(dff)
    acc = jnp.zeros(y_ref.shape, F32)
    for c in range(nchunk):
        halves = []
        for half in range(2):
            sl = slice(half * dff + c * tf, half * dff + (c + 1) * tf)
            u = jnp.dot(hn, wup_ref[:, sl], preferred_element_type=F32)
            ubuf_scr[0:pad, :] = carry_scr[:, sl]
            ubuf_scr[pad:pad + tm, :] = u
            cw = cw_ref[:, sl]
            conv = cb_ref[:, sl] + cw[FFN_CONV - 1:FFN_CONV] * u
            for kk in range(FFN_CONV - 1):
                start = pad - (FFN_CONV - 1) + kk
                conv = conv + cw[kk:kk + 1] * ubuf_scr[start:start + tm, :]
            carry_scr[:, sl] = ubuf_scr[tm:tm + pad, :]
            halves.append(conv)
        act = (_silu(halves[1]) * halves[0]).astype(BF16)
        acc = acc + jnp.dot(act, wdn_ref[c * tf:(c + 1) * tf, :], preferred_element_type=F32)
    y_ref[...] = x1_ref[...] + acc

    @pl.when(t == pl.num_programs(1) - 1)
    def _():
        tail_ref[0] = carry_scr[...]


def _ffn_prompt(hn, x1, wup, wdn, cw, cb, batch, seq, tm):
    d = x1.shape[1]
    dff = wdn.shape[0]
    nt = seq // tm
    tf, _ = _ffn_chunks(dff)
    row_spec = lambda n: pl.BlockSpec((tm, n), lambda b, t: (b * nt + t, 0))
    return pl.pallas_call(
        functools.partial(_ffn_prompt_body, tm=tm, dff=dff),
        grid=(batch, nt),
        in_specs=[row_spec(d), row_spec(d)] + [_resident(t.shape) for t in (wup, wdn, cw, cb)],
        out_specs=[row_spec(d), pl.BlockSpec((1, SUBLANES, 2 * dff), lambda b, t: (b, 0, 0))],
        out_shape=[jax.ShapeDtypeStruct(x1.shape, F32), jax.ShapeDtypeStruct((batch, SUBLANES, 2 * dff), F32)],
        scratch_shapes=[pltpu.VMEM((SUBLANES, 2 * dff), F32), pltpu.VMEM((SUBLANES + tm, tf), F32)],
        compiler_params=_params(("parallel", "arbitrary"), 52),
        name="ffn_prompt",
    )(hn, x1, wup, wdn, cw, cb)


def _ffn_sample_body(hn_ref, x1_ref, s0_ref, s1_ref, wup_ref, wdn_ref, cw_ref, cb_ref, y_ref, u_ref, *, dff):
    hn = hn_ref[...]
    tf, nchunk = _ffn_chunks(dff)
    acc = jnp.zeros(y_ref.shape, F32)
    for c in range(nchunk):
        halves = []
        for half in range(2):
            sl = slice(half * dff + c * tf, half * dff + (c + 1) * tf)
            u = jnp.dot(hn, wup_ref[:, sl], preferred_element_type=F32)
            u_ref[:, sl] = u
            cw = cw_ref[:, sl]
            halves.append(cb_ref[:, sl] + cw[0:1] * s0_ref[:, sl] + cw[1:2] * s1_ref[:, sl] + cw[2:3] * u)
        act = (_silu(halves[1]) * halves[0]).astype(BF16)
        acc = acc + jnp.dot(act, wdn_ref[c * tf:(c + 1) * tf, :], preferred_element_type=F32)
    y_ref[...] = x1_ref[...] + acc


def _ffn_sample(hn, x1, s0, s1, wup, wdn, cw, cb):
    dff = wdn.shape[0]
    return pl.pallas_call(
        functools.partial(_ffn_sample_body, dff=dff),
        out_shape=[jax.ShapeDtypeStruct(x1.shape, F32), jax.ShapeDtypeStruct((x1.shape[0], 2 * dff), F32)],
        compiler_params=pltpu.CompilerParams(vmem_limit_bytes=52 * MIB),
        name="ffn_sample",
    )(hn, x1, s0, s1, wup, wdn, cw, cb)


def _largest_tile(n, cap):
    t = cap
    while n % t:
        t //= 2
    return t


def kernel(x_prompt, x_sample, cache_k, cache_v, state_conv, state_ssm, state_ffn_conv, page_table, norm_mix_g, w_in, ssm_conv_w, ssm_conv_b, ssm_dt_bias, ssm_a_log, ssm_d, ssm_norm_g, q_norm_g, k_norm_g, lambda_q1, lambda_k1, lambda_q2, lambda_k2, attn_subln_g, w_ssd_o, w_attn_o, w_o, norm_ffn_g, w_ffn_up, ffn_conv_w, ffn_conv_b, w_ffn_down):
    bp, sp, d = x_prompt.shape
    bs, ss, _ = x_sample.shape
    assert ss == 1, "the sample group decodes one token per sequence"
    depth = w_in.shape[0]
    past = page_table.shape[1] * PAGE_SIZE
    d_inner = w_ssd_o.shape[1]
    n_heads = ssm_dt_bias.shape[1]
    conv_dim = ssm_conv_w.shape[2]
    att_w = w_attn_o.shape[1]
    kw = KV_HEADS * 2 * QK_DIM
    vw = KV_HEADS * V_DIM
    dff = w_ffn_down.shape[1]
    heads_g = n_heads // SSM_GROUPS
    nh_g = d_inner // SSM_GROUPS

    sizes = (d_inner, conv_dim, n_heads, att_w, kw, vw, d, d)
    cuts = np.concatenate([[0], np.cumsum(sizes)])
    order = (0, 1, 3, 4, 5, 6, 7, 2)
    widths = tuple(sizes[i] for i in order[:-1]) + (LANES,)

    tables_p = _rope_tables(jnp.arange(sp))
    tables_s = _rope_tables(jnp.broadcast_to(past + jnp.arange(ss), (bs,)))
    bd = jnp.asarray(np.kron(np.eye(LANES // QK_DIM), np.ones((QK_DIM, QK_DIM))), BF16)
    expand = jnp.asarray(np.pad(np.kron(np.eye(n_heads), np.ones((1, SSM_HEAD_DIM))),
                                ((0, LANES - n_heads), (0, 0))), F32)
    pad_heads = lambda v: jnp.pad(v.astype(F32), (0, LANES - n_heads)).reshape(1, LANES)

    xp = x_prompt.reshape(bp * sp, d)
    xs_ = x_sample.reshape(bs * ss, d)
    tm_p = _largest_tile(bp * sp, 256)
    tq = _largest_tile(sp, 256)
    tm_f = _largest_tile(sp, 512)

    outs = [[] for _ in range(10)]
    for layer in range(depth):
        lam_init = 0.8 - 0.6 * math.exp(-0.3 * layer)
        wl = w_in[layer]
        w_perm = jnp.concatenate([wl[:, cuts[i]:cuts[i + 1]] for i in order]
                                 + [jnp.zeros((d, LANES - n_heads), F32)], axis=1).astype(BF16)
        g_mix = norm_mix_g[layer].reshape(1, d)
        cw, cb = ssm_conv_w[layer], ssm_conv_b[layer].reshape(1, conv_dim)
        dtb = pad_heads(ssm_dt_bias[layer])
        a_neg = pad_heads(-jnp.exp(ssm_a_log[layer].astype(F32)))
        dsk = jnp.repeat(ssm_d[layer].astype(F32), SSM_HEAD_DIM).reshape(1, d_inner)
        ng = ssm_norm_g[layer].reshape(1, d_inner)
        qg = jnp.tile(q_norm_g[layer], LANES // QK_DIM).reshape(1, LANES)
        kg = jnp.tile(k_norm_g[layer], LANES // QK_DIM).reshape(1, LANES)
        lams = tuple(t[layer].reshape(1, QK_DIM) for t in (lambda_q1, lambda_k1, lambda_q2, lambda_k2))
        sg = attn_subln_g[layer].reshape(1, V_DIM)
        wso, wao, wo = (t[layer].astype(BF16) for t in (w_ssd_o, w_attn_o, w_o))
        g_ffn = norm_ffn_g[layer].reshape(1, d)
        wup, wdn = w_ffn_up[layer].astype(BF16), w_ffn_down[layer].astype(BF16)
        fcw, fcb = ffn_conv_w[layer], ffn_conv_b[layer].reshape(1, 2 * dff)

        z, xbc, qb, kf, kb, v, vt, gs, ga, dt = _in_proj(xp, g_mix, w_perm, tables_p, qg, kg, bd, widths, tm_p, sp)
        y_ssd, ctail, st = _ssd_prompt(z, xbc, dt, cw, cb, dtb, a_neg, expand, dsk, ng, bp, sp)
        vt = vt.reshape(bp * KV_HEADS * (V_DIM + ONES_ROWS), sp)
        y_att = _attn_prompt(lams, sg.reshape(V_DIM, 1), qb, kb, vt, bp, sp, lam_init, tq)
        x1, hn = _out_proj(y_ssd, y_att, gs, ga, xp, wso, wao, wo, g_ffn, tm_p)
        xp, ftail = _ffn_prompt(hn, x1, wup, wdn, fcw, fcb, bp, sp, tm_f)
        outs[0].append(kf.reshape(bp, sp, KV_HEADS, 2 * QK_DIM))
        outs[1].append(v.reshape(bp, sp, KV_HEADS, V_DIM))
        outs[2].append(ctail[:, SUBLANES - (SSM_CONV - 1):, :])
        outs[3].append(st.reshape(bp, SSM_GROUPS, SSM_STATE, heads_g, SSM_HEAD_DIM).transpose(0, 1, 3, 4, 2))
        outs[4].append(ftail[:, SUBLANES - (FFN_CONV - 1):, :])

        z, xbc, qb, kf, _, v, _, gs, ga, dt = _in_proj(xs_, g_mix, w_perm, tables_s, qg, kg, bd, widths, bs, bs)
        sc = state_conv[layer]
        xs_c, bm, cm, xdt, dec = _ssd_sample_pre(xbc, sc[:, 0], sc[:, 1], sc[:, 2], dt, cw, cb, dtb, a_neg, expand)
        h_new, y_ssd = _ssd_sample_step(state_ssm, layer, xdt, dec, bm, cm, xs_c, z, dsk, ng)
        y_att = _attn_sample(page_table, lams, sg, qb, kf, v, cache_k, cache_v, layer, lam_init)
        x1, hn = _out_proj(y_ssd.reshape(bs, d_inner), y_att.reshape(bs, att_w), gs, ga, xs_, wso, wao, wo, g_ffn, bs)
        sf = state_ffn_conv[layer]
        xs_, u_new = _ffn_sample(hn, x1, sf[:, 0], sf[:, 1], wup, wdn, fcw, fcb)
        outs[5].append(kf.reshape(bs, ss, KV_HEADS, 2 * QK_DIM))
        outs[6].append(v.reshape(bs, ss, KV_HEADS, V_DIM))
        outs[7].append(jnp.concatenate([sc[:, 1:], xbc[:, None, :]], axis=1))
        outs[8].append(h_new)
        outs[9].append(jnp.concatenate([sf[:, 1:], u_new[:, None, :]], axis=1))

    stacked = [jnp.stack(o) for o in outs]
    return (xp.reshape(bp, sp, d), xs_.reshape(bs, ss, d), *stacked)
```

```python
import functools
import math

import jax
import jax.numpy as jnp
import numpy as np
from jax import lax
from jax.experimental import pallas as pl
from jax.experimental.pallas import tpu as pltpu

F32 = jnp.float32
BF16 = jnp.bfloat16
HIGHEST = lax.Precision.HIGHEST

EPS = 1e-6
PAGE_SIZE = 128
SSM_HEAD_DIM = 64
SSM_GROUPS = 2
SSM_STATE = 64
SSM_CONV = 4
SSM_CHUNK = 128
KV_HEADS = 4
KV_REP = 2
QK_DIM = 64
V_DIM = 128
ROT_DIM = QK_DIM // 4
ROPE_THETA = 500000.0
FFN_CONV = 3
QK_SCALE = QK_DIM ** -0.5 * math.log2(math.e)
ONES_ROWS = 16
LANES = 128
SUBLANES = 8
PAGES_PER_STEP = 16
MIB = 1024 * 1024

NT_DIMS = (((1,), (1,)), ((), ()))
TN_DIMS = (((0,), (0,)), ((), ()))


def _params(semantics, vmem_mib):
    return pltpu.CompilerParams(dimension_semantics=semantics, vmem_limit_bytes=vmem_mib * MIB)


def _resident(shape):
    nd = len(shape)
    return pl.BlockSpec(shape, lambda *_: (0,) * nd, pipeline_mode=pl.Buffered(1))


def _sigmoid(x):
    return 1.0 / (1.0 + jnp.exp(-x))


def _silu(x):
    return x * _sigmoid(x)


def _softplus(x):
    return jnp.maximum(x, 0.0) + jnp.log1p(jnp.exp(-jnp.abs(x)))


def _split3(x):
    hi = x.astype(BF16)
    r1 = x - hi.astype(F32)
    mid = r1.astype(BF16)
    lo = (r1 - mid.astype(F32)).astype(BF16)
    return hi, mid, lo


def _select_matmul(x, sel):
    hi, mid, lo = _split3(x)
    return (jnp.dot(hi, sel, preferred_element_type=F32) + jnp.dot(mid, sel, preferred_element_type=F32)
            + jnp.dot(lo, sel, preferred_element_type=F32))


def _rms(x, axis=-1):
    return x * lax.rsqrt(jnp.mean(x * x, axis=axis, keepdims=True) + EPS)


def _inproj_body(x_ref, g_ref, w_ref, cos_ref, sa_ref, sb_ref, qg_ref, kg_ref, bd_ref,
                 z_ref, xbc_ref, qb_ref, kf_ref, kb_ref, v_ref, vt_ref, gs_ref, ga_ref, dt_ref):
    h = (_rms(x_ref[...]) * g_ref[...]).astype(BF16)
    cos, sa, sb, bd = cos_ref[...], sa_ref[...], sb_ref[...], bd_ref[...]
    off = [0]

    def proj(n):
        u = jnp.dot(h, w_ref[:, off[0]:off[0] + n], preferred_element_type=F32)
        off[0] += n
        return u

    def norm_rope(xb, g):
        ms = jnp.dot((xb * xb).astype(BF16), bd, preferred_element_type=F32) * (1.0 / QK_DIM)
        y = xb * lax.rsqrt(ms + EPS) * g
        return y * cos + pltpu.roll(y, LANES - ROT_DIM // 2, 1) * sa + pltpu.roll(y, ROT_DIM // 2, 1) * sb

    z_ref[...] = proj(z_ref.shape[-1])
    xbc_ref[...] = proj(xbc_ref.shape[-1])
    q = proj(qb_ref.shape[-1])
    for j in range(q.shape[-1] // LANES):
        sl = slice(j * LANES, (j + 1) * LANES)
        qb_ref[:, sl] = (norm_rope(q[:, sl], qg_ref[...]) * QK_SCALE).astype(BF16)
    tm = x_ref.shape[0]
    k = proj(kb_ref.shape[-1])
    for hd in range(KV_HEADS):
        sl = slice(hd * LANES, (hd + 1) * LANES)
        kr = norm_rope(k[:, sl], kg_ref[...])
        kf_ref[pl.ds(hd, tm, stride=KV_HEADS), :] = kr
        kb_ref[:, sl] = kr.astype(BF16)
    v = proj(KV_HEADS * V_DIM)
    ones = jnp.ones((ONES_ROWS, tm), BF16)
    for hd in range(KV_HEADS):
        vh = v[:, hd * V_DIM:(hd + 1) * V_DIM]
        v_ref[pl.ds(hd, tm, stride=KV_HEADS), :] = vh
        vt_ref[0, hd, :V_DIM, :] = vh.T.astype(BF16)
        vt_ref[0, hd, V_DIM:, :] = ones
    gs_ref[...] = proj(gs_ref.shape[-1])
    ga_ref[...] = proj(ga_ref.shape[-1])
    dt_ref[...] = proj(dt_ref.shape[-1])


def _in_proj(x, g, w, tables, qg, kg, bd, widths, tm, seq):
    rows, d = x.shape
    nt = seq // tm
    wz, wxbc, wq, wk, wv, wgs, wga, wdt = widths
    row = lambda n: pl.BlockSpec((tm, n), lambda i: (i, 0))
    tab = pl.BlockSpec((tm, LANES), lambda i: (i % nt, 0))
    f32 = lambda n: jax.ShapeDtypeStruct((rows, n), F32)
    b16 = lambda n: jax.ShapeDtypeStruct((rows, n), BF16)
    vt_rows = V_DIM + ONES_ROWS
    assert wk == KV_HEADS * LANES and wv == KV_HEADS * V_DIM and V_DIM == LANES
    head_rows = pl.BlockSpec((tm * KV_HEADS, LANES), lambda i: (i, 0))
    head_rows_shape = jax.ShapeDtypeStruct((rows * KV_HEADS, LANES), F32)
    return pl.pallas_call(
        _inproj_body,
        grid=(rows // tm,),
        in_specs=[row(d), _resident(g.shape), _resident(w.shape), tab, tab, tab,
                  _resident(qg.shape), _resident(kg.shape), _resident(bd.shape)],
        out_specs=[row(wz), row(wxbc), row(wq), head_rows, row(wk), head_rows,
                   pl.BlockSpec((1, KV_HEADS, vt_rows, tm), lambda i: (i // nt, 0, 0, i % nt)),
                   row(wgs), row(wga), row(wdt)],
        out_shape=[f32(wz), f32(wxbc), b16(wq), head_rows_shape, b16(wk), head_rows_shape,
                   jax.ShapeDtypeStruct((rows // seq, KV_HEADS, vt_rows, seq), BF16),
                   f32(wgs), f32(wga), f32(wdt)],
        compiler_params=_params(("parallel",), 52),
        name="in_proj",
    )(x, g, w, *tables, qg, kg, bd)


def _rope_tables(pos):
    half = ROT_DIM // 2
    inv = ROPE_THETA ** (-jnp.arange(half, dtype=F32) * 2.0 / ROT_DIM)
    ang = pos.astype(F32)[:, None] * inv[None, :]
    cos, sin = jnp.cos(ang), jnp.sin(ang)
    n = pos.shape[0]
    rest = QK_DIM - ROT_DIM
    cos_h = jnp.concatenate([cos, cos, jnp.ones((n, rest), F32)], axis=1)
    sa_h = jnp.concatenate([-sin, jnp.zeros((n, half + rest), F32)], axis=1)
    sb_h = jnp.concatenate([jnp.zeros((n, half), F32), sin, jnp.zeros((n, rest), F32)], axis=1)
    rep = LANES // QK_DIM
    return tuple(jnp.tile(t, (1, rep)) for t in (cos_h, sa_h, sb_h))


def _ssd_body(z_ref, xbc_ref, dt_ref, cw_ref, cb_ref, dtb_ref, a_ref, e_ref, dsk_ref, ng_ref,
              y_ref, ctail_ref, st_ref, xpad_scr, h_scr, *, ch, d_inner):
    c = pl.program_id(1)
    nh_g = d_inner // SSM_GROUPS
    pad = SUBLANES

    @pl.when(c == 0)
    def _():
        xpad_scr[0:pad, :] = jnp.zeros((pad, xpad_scr.shape[1]), F32)
        h_scr[...] = jnp.zeros(h_scr.shape, F32)

    xbc = xbc_ref[...]
    xpad_scr[pad:pad + ch, :] = xbc
    cw = cw_ref[...]
    conv = cb_ref[...] + cw[SSM_CONV - 1:SSM_CONV] * xbc
    for kk in range(SSM_CONV - 1):
        start = pad - (SSM_CONV - 1) + kk
        conv = conv + cw[kk:kk + 1] * xpad_scr[start:start + ch, :]
    xpad_scr[0:pad, :] = xpad_scr[ch:ch + pad, :]
    act = _silu(conv)
    xs = act[:, :d_inner]
    bmb = act[:, d_inner:d_inner + LANES].astype(BF16)
    cmb = act[:, d_inner + LANES:d_inner + 2 * LANES].astype(BF16)

    dt = _softplus(dt_ref[...] + dtb_ref[...])
    da = dt * a_ref[...]
    row = lax.broadcasted_iota(jnp.int32, (ch, ch), 0)
    col = lax.broadcasted_iota(jnp.int32, (ch, ch), 1)
    causal = row >= col
    da3 = jnp.concatenate(_split3(da), axis=1)
    cs3 = jnp.dot(causal.astype(BF16), da3, preferred_element_type=F32)
    cs = cs3[:, :LANES] + cs3[:, LANES:2 * LANES] + cs3[:, 2 * LANES:]
    cs_t = cs.T
    e = e_ref[...]
    cs_e = _select_matmul(cs, e)
    dt_e = _select_matmul(dt, e)
    xdt = xs * dt_e
    last = cs_e[ch - 1:ch, :]
    xdtb = xdt.astype(BF16)
    wendb = (xdt * jnp.exp(last - cs_e)).astype(BF16)
    off_scale = jnp.exp(cs_e)
    cdec = jnp.exp(last)

    lane = lax.broadcasted_iota(jnp.int32, (ch, LANES), 1)
    lo = lane < SSM_HEAD_DIM
    zero_b = jnp.zeros((ch, LANES), BF16)
    heads_g = nh_g // SSM_HEAD_DIM
    ys = []
    for g in range(SSM_GROUPS):
        bg = bmb[:, g * SSM_STATE:(g + 1) * SSM_STATE]
        cg = cmb[:, g * SSM_STATE:(g + 1) * SSM_STATE]
        cb = lax.dot_general(cg, bg, NT_DIMS, preferred_element_type=F32)
        hg = h_scr[g]
        yoff = jnp.dot(cg, hg.astype(BF16), preferred_element_type=F32)
        s_new = lax.dot_general(bg, wendb[:, g * nh_g:(g + 1) * nh_g], TN_DIMS, preferred_element_type=F32)
        h_scr[g] = hg * cdec[:, g * nh_g:(g + 1) * nh_g] + s_new
        for pr in range(heads_g // 2):
            r0 = g * heads_g + 2 * pr
            gs = []
            for r in (r0, r0 + 1):
                seg = cs[:, r:r + 1] - cs_t[r:r + 1, :]
                gs.append(jnp.where(causal, cb * jnp.exp(seg), 0.0))
            gc = jnp.concatenate(gs, axis=1).astype(BF16)
            blk = xdtb[:, r0 * SSM_HEAD_DIM:(r0 + 2) * SSM_HEAD_DIM]
            rhs = jnp.concatenate([jnp.where(lo, blk, zero_b), jnp.where(lo, zero_b, blk)], axis=0)
            yd = jnp.dot(gc, rhs, preferred_element_type=F32)
            sl = slice(r0 * SSM_HEAD_DIM, (r0 + 2) * SSM_HEAD_DIM)
            ys.append(yd + yoff[:, 2 * pr * SSM_HEAD_DIM:(2 * pr + 2) * SSM_HEAD_DIM] * off_scale[:, sl])
    y = jnp.concatenate(ys, axis=1) + xs * dsk_ref[...]
    yg = y * _silu(z_ref[...])
    outs = [_rms(yg[:, g * nh_g:(g + 1) * nh_g]) for g in range(SSM_GROUPS)]
    y_ref[...] = (jnp.concatenate(outs, axis=1) * ng_ref[...]).astype(BF16)

    @pl.when(c == pl.num_programs(1) - 1)
    def _():
        ctail_ref[0] = xpad_scr[0:pad, :]
        st_ref[0] = h_scr[...]


def _ssd_prompt(z, xbc, dt, cw, cb, dtb, a, e, dsk, ng, batch, seq):
    ch = SSM_CHUNK
    nc = seq // ch
    d_inner = z.shape[1]
    conv_dim = xbc.shape[1]
    nh_g = d_inner // SSM_GROUPS
    row_spec = lambda n: pl.BlockSpec((ch, n), lambda b, c: (b * nc + c, 0))
    return pl.pallas_call(
        functools.partial(_ssd_body, ch=ch, d_inner=d_inner),
        grid=(batch, nc),
        in_specs=[row_spec(d_inner), row_spec(conv_dim), row_spec(LANES)]
        + [_resident(t.shape) for t in (cw, cb, dtb, a, e, dsk, ng)],
        out_specs=[row_spec(d_inner),
                   pl.BlockSpec((1, SUBLANES, conv_dim), lambda b, c: (b, 0, 0)),
                   pl.BlockSpec((1, SSM_GROUPS, SSM_STATE, nh_g), lambda b, c: (b, 0, 0, 0))],
        out_shape=[jax.ShapeDtypeStruct((batch * seq, d_inner), BF16),
                   jax.ShapeDtypeStruct((batch, SUBLANES, conv_dim), F32),
                   jax.ShapeDtypeStruct((batch, SSM_GROUPS, SSM_STATE, nh_g), F32)],
        scratch_shapes=[pltpu.VMEM((SUBLANES + ch, conv_dim), F32),
                        pltpu.VMEM((SSM_GROUPS, SSM_STATE, nh_g), F32)],
        compiler_params=_params(("parallel", "arbitrary"), 40),
        name="ssd_prompt",
    )(z, xbc, dt, cw, cb, dtb, a, e, dsk, ng)


def _lambda(lq1_ref, lk1_ref, lq2_ref, lk2_ref, lam_init):
    s1 = jnp.sum(lq1_ref[...] * lk1_ref[...], axis=-1, keepdims=True)
    s2 = jnp.sum(lq2_ref[...] * lk2_ref[...], axis=-1, keepdims=True)
    return jnp.exp(s1) - jnp.exp(s2) + lam_init


def _attn_body(lq1_ref, lk1_ref, lq2_ref, lk2_ref, sg_ref, q_ref, k_ref, vt_ref, o_ref,
               qp_scr, m_scr, acc_scr, sa_scr, sb_scr, *, tq, lam_init):
    qi = pl.program_id(2)
    q = q_ref[...]
    lane = lax.broadcasted_iota(jnp.int32, (tq, LANES), 1)
    lo = lane < QK_DIM
    zero_b = jnp.zeros((tq, LANES), BF16)
    for r in range(KV_REP):
        qr = q[:, r * LANES:(r + 1) * LANES]
        qp_scr[(2 * r) * tq:(2 * r + 1) * tq, :] = jnp.where(lo, qr, zero_b)
        qp_scr[(2 * r + 1) * tq:(2 * r + 2) * tq, :] = jnp.where(lo, zero_b, qr)
    m_scr[...] = jnp.full(m_scr.shape, -jnp.inf, F32)
    acc_scr[...] = jnp.zeros(acc_scr.shape, F32)

    def scores(j, s_scr):
        start = pl.multiple_of(j * tq, tq)
        s_scr[...] = lax.dot_general(k_ref[pl.ds(start, tq), :], qp_scr[...], NT_DIMS,
                                     preferred_element_type=F32)

    def consume(j, s_scr, diagonal):
        start = pl.multiple_of(j * tq, tq)
        vtb = vt_ref[:, pl.ds(start, tq)]
        m_prev = m_scr[...]
        ps, ms = [], []
        for c in range(2 * KV_REP * tq // LANES):
            cols = slice(c * LANES, (c + 1) * LANES)
            s = s_scr[:, cols]
            if diagonal:
                key = lax.broadcasted_iota(jnp.int32, (tq, LANES), 0)
                qry = lax.broadcasted_iota(jnp.int32, (tq, LANES), 1) + (c * LANES) % tq
                s = jnp.where(key <= qry, s, -jnp.inf)
            m_c = jnp.maximum(m_prev[:, cols], jnp.max(s, axis=0, keepdims=True))
            ps.append(jnp.exp2(s - m_c).astype(BF16))
            ms.append(m_c)
        m_new = jnp.concatenate(ms, axis=1)
        alpha = jnp.exp2(m_prev - m_new)
        pv = jnp.dot(vtb, jnp.concatenate(ps, axis=1), preferred_element_type=F32)
        acc_scr[...] = alpha * acc_scr[...] + pv
        m_scr[...] = m_new

    def pair(i, carry):
        scores(2 * i + 1, sb_scr)
        consume(2 * i, sa_scr, False)
        scores(2 * i + 2, sa_scr)
        consume(2 * i + 1, sb_scr, False)
        return carry

    scores(0, sa_scr)
    lax.fori_loop(0, qi // 2, pair, 0)

    @pl.when(qi % 2 == 1)
    def _():
        scores(qi, sb_scr)
        consume(qi - 1, sa_scr, False)
        consume(qi, sb_scr, True)

    @pl.when(qi % 2 == 0)
    def _():
        consume(qi, sa_scr, True)

    lam = _lambda(lq1_ref, lk1_ref, lq2_ref, lk2_ref, lam_init)
    for r in range(KV_REP):
        c1 = slice((2 * r) * tq, (2 * r + 1) * tq)
        c2 = slice((2 * r + 1) * tq, (2 * r + 2) * tq)
        o = (acc_scr[:V_DIM, c1] / acc_scr[V_DIM:V_DIM + 1, c1]
             - lam * (acc_scr[:V_DIM, c2] / acc_scr[V_DIM:V_DIM + 1, c2]))
        y = _rms(o, axis=0) * sg_ref[...] * (1.0 - lam_init)
        o_ref[:, r * LANES:(r + 1) * LANES] = y.T.astype(BF16)


def _attn_prompt(lams, sg_col, qb, kb, vt, batch, seq, lam_init, tq):
    nq = seq // tq
    cols = 2 * KV_REP * tq
    att_w = qb.shape[1]
    return pl.pallas_call(
        functools.partial(_attn_body, tq=tq, lam_init=lam_init),
        grid=(batch, KV_HEADS, nq),
        in_specs=[_resident(t.shape) for t in lams] + [_resident(sg_col.shape)] + [
            pl.BlockSpec((tq, KV_REP * LANES), lambda b, h, i: (b * nq + i, h)),
            pl.BlockSpec((seq, LANES), lambda b, h, i: (b, h)),
            pl.BlockSpec((V_DIM + ONES_ROWS, seq), lambda b, h, i: (b * KV_HEADS + h, 0))],
        out_specs=pl.BlockSpec((tq, KV_REP * LANES), lambda b, h, i: (b * nq + i, h)),
        out_shape=jax.ShapeDtypeStruct((batch * seq, att_w), BF16),
        scratch_shapes=[pltpu.VMEM((cols, LANES), BF16), pltpu.VMEM((1, cols), F32),
                        pltpu.VMEM((V_DIM + ONES_ROWS, cols), F32), pltpu.VMEM((tq, cols), F32),
                        pltpu.VMEM((tq, cols), F32)],
        compiler_params=_params(("parallel", "parallel", "arbitrary"), 40),
        name="attn_prompt",
    )(*lams, sg_col, qb, kb, vt)


def _ssd_pre_body(xbc_ref, s0_ref, s1_ref, s2_ref, dt_ref, cw_ref, cb_ref, dtb_ref, a_ref, e_ref,
                  xs_ref, bm_ref, cm_ref, xdt_ref, dec_ref, *, d_inner):
    cw = cw_ref[...]
    conv = (cb_ref[...] + cw[0:1] * s0_ref[...] + cw[1:2] * s1_ref[...] + cw[2:3] * s2_ref[...]
            + cw[3:4] * xbc_ref[...])
    act = _silu(conv)
    xs = act[:, :d_inner]
    dt = _softplus(dt_ref[...] + dtb_ref[...])
    e = e_ref[...]
    dt_e = _select_matmul(dt, e)
    dec_e = _select_matmul(jnp.exp(dt * a_ref[...]), e)
    xs_ref[...] = xs
    bm_ref[...] = act[:, d_inner:d_inner + LANES]
    cm_ref[...] = act[:, d_inner + LANES:d_inner + 2 * LANES]
    xdt_ref[...] = xs * dt_e
    dec_ref[...] = dec_e


def _ssd_sample_pre(xbc, s0, s1, s2, dt, cw, cb, dtb, a, e):
    n, d_inner = xbc.shape[0], e.shape[1]
    wide = jax.ShapeDtypeStruct((n, d_inner), F32)
    narrow = jax.ShapeDtypeStruct((n, LANES), F32)
    return pl.pallas_call(
        functools.partial(_ssd_pre_body, d_inner=d_inner),
        out_shape=[wide, narrow, narrow, wide, wide],
        compiler_params=pltpu.CompilerParams(vmem_limit_bytes=32 * MIB),
        name="ssd_sample_pre",
    )(xbc, s0, s1, s2, dt, cw, cb, dtb, a, e)


def _ssd_step_body(h0_ref, xdt_ref, dec_ref, bm_ref, cm_ref, xs_ref, z_ref, dsk_ref, ng_ref,
                   hn_ref, y_ref, *, nh_g):
    xdt, dec, bm, cm = xdt_ref[0], dec_ref[0], bm_ref[0], cm_ref[0]
    pad7 = lambda v: jnp.concatenate([v, jnp.zeros((SUBLANES - 1, v.shape[1]), F32)], axis=0)
    ones_row = pad7(jnp.ones((1, SSM_STATE), F32))
    ys = []
    for g in range(SSM_GROUPS):
        h0 = h0_ref[0, 0, g].reshape(nh_g, SSM_STATE)
        bg = bm[:, g * SSM_STATE:(g + 1) * SSM_STATE]
        cg = cm[:, g * SSM_STATE:(g + 1) * SSM_STATE]
        xg = xdt[:, g * nh_g:(g + 1) * nh_g]
        dg = dec[:, g * nh_g:(g + 1) * nh_g]
        outer = lax.dot_general(pad7(xg), pad7(bg), TN_DIMS, precision=HIGHEST, preferred_element_type=F32)
        dcol = lax.dot_general(pad7(dg), ones_row, TN_DIMS, precision=HIGHEST, preferred_element_type=F32)
        hn_ref[0, g] = (h0 * dcol + outer).reshape(hn_ref.shape[2:])
        ch0 = lax.dot_general(pad7(cg), h0, NT_DIMS, precision=HIGHEST, preferred_element_type=F32)[0:1]
        ys.append(dg * ch0 + xg * jnp.sum(cg * bg, axis=-1, keepdims=True))
    y = jnp.concatenate(ys, axis=1) + xs_ref[0] * dsk_ref[...]
    yg = y * _silu(z_ref[0])
    outs = [_rms(yg[:, g * nh_g:(g + 1) * nh_g]) for g in range(SSM_GROUPS)]
    y_ref[0] = (jnp.concatenate(outs, axis=1) * ng_ref[...]).astype(BF16)


def _ssd_sample_step(state, layer, xdt, dec, bm, cm, xs, z, dsk, ng):
    n = state.shape[1]
    st_dims = state.shape[2:]
    d_inner = xs.shape[-1]
    row3 = lambda w: pl.BlockSpec((1, 1, w), lambda b: (b, 0, 0))
    r3 = lambda t: t.reshape(n, 1, t.shape[-1])
    return pl.pallas_call(
        functools.partial(_ssd_step_body, nh_g=d_inner // SSM_GROUPS),
        grid=(n,),
        in_specs=[pl.BlockSpec((1, 1) + st_dims, lambda b: (layer, b, 0, 0, 0, 0)),
                  row3(d_inner), row3(d_inner), row3(LANES), row3(LANES), row3(d_inner), row3(d_inner),
                  _resident(dsk.shape), _resident(ng.shape)],
        out_specs=[pl.BlockSpec((1,) + st_dims, lambda b: (b, 0, 0, 0, 0)), row3(d_inner)],
        out_shape=[jax.ShapeDtypeStruct((n,) + st_dims, F32), jax.ShapeDtypeStruct((n, 1, d_inner), BF16)],
        compiler_params=_params(("parallel",), 32),
        name="ssd_sample_step",
    )(state, r3(xdt), r3(dec), r3(bm), r3(cm), r3(xs), r3(z), dsk, ng)


def _attn_sample_body(pt_ref, lq1_ref, lk1_ref, lq2_ref, lk2_ref, sg_ref, q_ref, kn_ref, vn_ref, *rest,
                      pages, lam_init):
    k_refs, v_refs = rest[:pages], rest[pages:2 * pages]
    o_ref, qp_scr, bias_scr, m_scr, l_scr, acc_scr = rest[2 * pages:]
    j = pl.program_id(1)
    prow = PAGE_SIZE * KV_HEADS

    def head_rows(row128):
        per_head = [jnp.broadcast_to(row128[:, h * LANES:(h + 1) * LANES], (2, LANES)) for h in range(KV_HEADS)]
        return jnp.concatenate(per_head * KV_REP, axis=0)

    @pl.when(j == 0)
    def _():
        q = q_ref[0].astype(F32)
        lane = lax.broadcasted_iota(jnp.int32, (2, LANES), 1)
        comp = lax.broadcasted_iota(jnp.int32, (2, LANES), 0)
        own_half = (lane // QK_DIM) == comp
        rows = []
        for r in range(KV_REP):
            for h in range(KV_HEADS):
                qhr = q[:, (KV_REP * h + r) * LANES:(KV_REP * h + r + 1) * LANES]
                rows.append(jnp.where(own_half, jnp.broadcast_to(qhr, (2, LANES)), 0.0))
        qp_scr[...] = jnp.concatenate(rows, axis=0).astype(BF16)
        row_head = (lax.broadcasted_iota(jnp.int32, bias_scr.shape, 0) // 2) % KV_HEADS
        col_head = lax.broadcasted_iota(jnp.int32, bias_scr.shape, 1) % KV_HEADS
        bias_scr[...] = jnp.where(row_head == col_head, 0.0, -jnp.inf)
        m_scr[...] = jnp.full(m_scr.shape, -jnp.inf, F32)
        l_scr[...] = jnp.zeros(l_scr.shape, F32)
        acc_scr[...] = jnp.zeros(acc_scr.shape, F32)

    qp = qp_scr[...]
    s = jnp.concatenate(
        [lax.dot_general(qp, k_refs[i][...].astype(BF16), NT_DIMS, preferred_element_type=F32)
         for i in range(pages)], axis=1) + bias_scr[...]
    m_prev = m_scr[...]
    m_new = jnp.maximum(m_prev, jnp.max(s, axis=-1, keepdims=True))
    alpha = jnp.exp2(m_prev - m_new)
    p = jnp.exp2(s - m_new)
    l_scr[...] = alpha * l_scr[...] + jnp.sum(p, axis=-1, keepdims=True)
    pb = p.astype(BF16)
    pv = jnp.dot(pb[:, 0:prow], v_refs[0][...].astype(BF16), preferred_element_type=F32)
    for i in range(1, pages):
        pv = pv + jnp.dot(pb[:, i * prow:(i + 1) * prow], v_refs[i][...].astype(BF16),
                          preferred_element_type=F32)
    acc_scr[...] = alpha * acc_scr[...] + pv
    m_scr[...] = m_new

    @pl.when(j == pl.num_programs(1) - 1)
    def _():
        k_self = head_rows(kn_ref[0]).astype(BF16).astype(F32)
        s_self = jnp.sum(qp.astype(F32) * k_self, axis=-1, keepdims=True)
        m_prev = m_scr[...]
        m_fin = jnp.maximum(m_prev, s_self)
        alpha = jnp.exp2(m_prev - m_fin)
        p_self = jnp.exp2(s_self - m_fin)
        l_fin = alpha * l_scr[...] + p_self
        acc = (alpha * acc_scr[...] + p_self * head_rows(vn_ref[0])) / l_fin
        lam = _lambda(lq1_ref, lk1_ref, lq2_ref, lk2_ref, lam_init)
        for h in range(KV_HEADS):
            for r in range(KV_REP):
                i1 = (r * KV_HEADS + h) * 2
                o = acc[i1:i1 + 1, :] - lam * acc[i1 + 1:i1 + 2, :]
                y = _rms(o) * sg_ref[...] * (1.0 - lam_init)
                dst = (KV_REP * h + r) * LANES
                o_ref[0, :, dst:dst + LANES] = y.astype(BF16)


def _attn_sample(page_table, lams, sg, qb, k_new, v_new, cache_k, cache_v, layer, lam_init):
    n, n_pages = page_table.shape
    pages = PAGES_PER_STEP
    while n_pages % pages:
        pages //= 2
    att_w = qb.shape[-1]
    kw = KV_HEADS * LANES
    nrow = 2 * KV_REP * KV_HEADS
    n_pool = cache_k.shape[1]
    prow = PAGE_SIZE * KV_HEADS
    ck = cache_k.reshape(cache_k.shape[0] * n_pool * prow, 2 * QK_DIM)
    cv = cache_v.reshape(cache_v.shape[0] * n_pool * prow, V_DIM)
    page_table = page_table + layer * n_pool
    const = lambda shape: pl.BlockSpec(shape, lambda b, j, pt: (0,) * len(shape))
    row3 = lambda w: pl.BlockSpec((1, 1, w), lambda b, j, pt: (b, 0, 0))

    def page_spec(i):
        return pl.BlockSpec((prow, LANES), lambda b, j, pt: (pt[b * n_pages + j * pages + i], 0))

    grid_spec = pltpu.PrefetchScalarGridSpec(
        num_scalar_prefetch=1,
        grid=(n, n_pages // pages),
        in_specs=[const(t.shape) for t in lams] + [const(sg.shape), row3(att_w), row3(kw), row3(kw)]
        + [page_spec(i) for i in range(pages)] + [page_spec(i) for i in range(pages)],
        out_specs=row3(att_w),
        scratch_shapes=[pltpu.VMEM((nrow, LANES), BF16), pltpu.VMEM((nrow, pages * prow), F32),
                        pltpu.VMEM((nrow, 1), F32), pltpu.VMEM((nrow, 1), F32), pltpu.VMEM((nrow, LANES), F32)],
    )
    return pl.pallas_call(
        functools.partial(_attn_sample_body, pages=pages, lam_init=lam_init),
        grid_spec=grid_spec,
        out_shape=jax.ShapeDtypeStruct((n, 1, att_w), BF16),
        compiler_params=_params(("parallel", "arbitrary"), 40),
        name="attn_sample",
    )(page_table.reshape(-1), *lams, sg, qb.reshape(n, 1, att_w), k_new.reshape(n, 1, kw),
      v_new.reshape(n, 1, kw), *([ck] * pages), *([cv] * pages))


def _outproj_body(ys_ref, ya_ref, gs_ref, ga_ref, x_ref, wso_ref, wao_ref, wo_ref, g_ref, x1_ref, hn_ref):
    o_ssd = jnp.dot(ys_ref[...], wso_ref[...], preferred_element_type=F32)
    o_att = jnp.dot(ya_ref[...], wao_ref[...], preferred_element_type=F32)
    merged = _sigmoid(gs_ref[...]) * o_ssd + _sigmoid(ga_ref[...]) * o_att
    x1 = x_ref[...] + jnp.dot(merged.astype(BF16), wo_ref[...], preferred_element_type=F32)
    x1_ref[...] = x1
    hn_ref[...] = (_rms(x1) * g_ref[...]).astype(BF16)


def _out_proj(ys, ya, gs, ga, x, wso, wao, wo, g, tm):
    rows, d = x.shape
    row_spec = lambda n: pl.BlockSpec((tm, n), lambda i: (i, 0))
    return pl.pallas_call(
        _outproj_body,
        grid=(rows // tm,),
        in_specs=[row_spec(ys.shape[1]), row_spec(ya.shape[1]), row_spec(d), row_spec(d), row_spec(d)]
        + [_resident(t.shape) for t in (wso, wao, wo, g)],
        out_specs=[row_spec(d), row_spec(d)],
        out_shape=[jax.ShapeDtypeStruct((rows, d), F32), jax.ShapeDtypeStruct((rows, d), BF16)],
        compiler_params=_params(("parallel",), 48),
        name="out_proj",
    )(ys, ya, gs, ga, x, wso, wao, wo, g)


def _ffn_chunks(dff):
    tf = 2 * LANES
    assert dff % tf == 0
    return tf, dff // tf


def _ffn_prompt_body(hn_ref, x1_ref, wup_ref, wdn_ref, cw_ref, cb_ref, y_ref, tail_ref,
                     ubuf_scr, act_scr, *, tm, dff):
    t = pl.program_id(1)
    pad = SUBLANES

    @pl.when(t == 0)
    def _():
        ubuf_scr[0:pad, :] = jnp.zeros((pad, ubuf_scr.shape[1]), F32)

    hn = hn_ref[...]
    tf, nchunk = _ffn_chunks(dff)
    for c in range(nchunk):
        halves = []
        for half in range(2):
            sl = slice(half * dff + c * tf, half * dff + (c + 1) * tf)
            u = jnp.dot(hn, wup_ref[:, sl], preferred_element_type=F32)
            ubuf_scr[pad:pad + tm, sl] = u
            cw = cw_ref[:, sl]
            conv = cb_ref[:, sl] + cw[FFN_CONV - 1:FFN_CONV] * u
            for kk in range(FFN_CONV - 1):
                start = pad - (FFN_CONV - 1) + kk
                conv = conv + cw[kk:kk + 1] * ubuf_scr[start:start + tm, sl]
            ubuf_scr[0:pad, sl] = ubuf_scr[tm:tm + pad, sl]
            halves.append(conv)
        act_scr[:, c * tf:(c + 1) * tf] = (_silu(halves[1]) * halves[0]).astype(BF16)
    y_ref[...] = x1_ref[...] + jnp.dot(act_scr[...], wdn_ref[...], preferred_element_type=F32)

    @pl.when(t == pl.num_programs(1) - 1)
    def _():
        tail_ref[0] = ubuf_scr[0:pad, :]


def _ffn_prompt(hn, x1, wup, wdn, cw, cb, batch, seq, tm):
    d = x1.shape[1]
    dff = wdn.shape[0]
    nt = seq // tm
    row_spec = lambda n: pl.BlockSpec((tm, n), lambda b, t: (b * nt + t, 0))
    return pl.pallas_call(
        functools.partial(_ffn_prompt_body, tm=tm, dff=dff),
        grid=(batch, nt),
        in_specs=[row_spec(d), row_spec(d)] + [_resident(t.shape) for t in (wup, wdn, cw, cb)],
        out_specs=[row_spec(d), pl.BlockSpec((1, SUBLANES, 2 * dff), lambda b, t: (b, 0, 0))],
        out_shape=[jax.ShapeDtypeStruct(x1.shape, F32), jax.ShapeDtypeStruct((batch, SUBLANES, 2 * dff), F32)],
        scratch_shapes=[pltpu.VMEM((SUBLANES + tm, 2 * dff), F32), pltpu.VMEM((tm, dff), BF16)],
        compiler_params=_params(("parallel", "arbitrary"), 52),
        name="ffn_prompt",
    )(hn, x1, wup, wdn, cw, cb)


def _ffn_sample_body(hn_ref, x1_ref, s0_ref, s1_ref, wup_ref, wdn_ref, cw_ref, cb_ref, y_ref, u_ref, *, dff):
    hn = hn_ref[...]
    tf, nchunk = _ffn_chunks(dff)
    acc = jnp.zeros(y_ref.shape, F32)
    for c in range(nchunk):
        halves = []
        for half in range(2):
            sl = slice(half * dff + c * tf, half * dff + (c + 1) * tf)
            u = jnp.dot(hn, wup_ref[:, sl], preferred_element_type=F32)
            u_ref[:, sl] = u
            cw = cw_ref[:, sl]
            halves.append(cb_ref[:, sl] + cw[0:1] * s0_ref[:, sl] + cw[1:2] * s1_ref[:, sl] + cw[2:3] * u)
        act = (_silu(halves[1]) * halves[0]).astype(BF16)
        acc = acc + jnp.dot(act, wdn_ref[c * tf:(c + 1) * tf, :], preferred_element_type=F32)
    y_ref[...] = x1_ref[...] + acc


def _ffn_sample(hn, x1, s0, s1, wup, wdn, cw, cb):
    dff = wdn.shape[0]
    return pl.pallas_call(
        functools.partial(_ffn_sample_body, dff=dff),
        out_shape=[jax.ShapeDtypeStruct(x1.shape, F32), jax.ShapeDtypeStruct((x1.shape[0], 2 * dff), F32)],
        compiler_params=pltpu.CompilerParams(vmem_limit_bytes=52 * MIB),
        name="ffn_sample",
    )(hn, x1, s0, s1, wup, wdn, cw, cb)


def _largest_tile(n, cap):
    t = cap
    while n % t:
        t //= 2
    return t


def kernel(x_prompt, x_sample, cache_k, cache_v, state_conv, state_ssm, state_ffn_conv, page_table, norm_mix_g, w_in, ssm_conv_w, ssm_conv_b, ssm_dt_bias, ssm_a_log, ssm_d, ssm_norm_g, q_norm_g, k_norm_g, lambda_q1, lambda_k1, lambda_q2, lambda_k2, attn_subln_g, w_ssd_o, w_attn_o, w_o, norm_ffn_g, w_ffn_up, ffn_conv_w, ffn_conv_b, w_ffn_down):
    bp, sp, d = x_prompt.shape
    bs, ss, _ = x_sample.shape
    assert ss == 1, "the sample group decodes one token per sequence"
    depth = w_in.shape[0]
    past = page_table.shape[1] * PAGE_SIZE
    d_inner = w_ssd_o.shape[1]
    n_heads = ssm_dt_bias.shape[1]
    conv_dim = ssm_conv_w.shape[2]
    att_w = w_attn_o.shape[1]
    kw = KV_HEADS * 2 * QK_DIM
    vw = KV_HEADS * V_DIM
    dff = w_ffn_down.shape[1]
    heads_g = n_heads // SSM_GROUPS
    nh_g = d_inner // SSM_GROUPS

    sizes = (d_inner, conv_dim, n_heads, att_w, kw, vw, d, d)
    cuts = np.concatenate([[0], np.cumsum(sizes)])
    order = (0, 1, 3, 4, 5, 6, 7, 2)
    widths = tuple(sizes[i] for i in order[:-1]) + (LANES,)

    tables_p = _rope_tables(jnp.arange(sp))
    tables_s = _rope_tables(jnp.broadcast_to(past + jnp.arange(ss), (bs,)))
    bd = jnp.asarray(np.kron(np.eye(LANES // QK_DIM), np.ones((QK_DIM, QK_DIM))), BF16)
    expand = jnp.asarray(np.pad(np.kron(np.eye(n_heads), np.ones((1, SSM_HEAD_DIM))),
                                ((0, LANES - n_heads), (0, 0))), BF16)
    pad_heads = lambda v: jnp.pad(v.astype(F32), (0, LANES - n_heads)).reshape(1, LANES)

    xp = x_prompt.reshape(bp * sp, d)
    xs_ = x_sample.reshape(bs * ss, d)
    tm_p = _largest_tile(bp * sp, 256)
    tq = _largest_tile(sp, 256)
    tm_f = _largest_tile(sp, 256)

    outs = [[] for _ in range(10)]
    for layer in range(depth):
        lam_init = 0.8 - 0.6 * math.exp(-0.3 * layer)
        wl = w_in[layer]
        w_perm = jnp.concatenate([wl[:, cuts[i]:cuts[i + 1]] for i in order]
                                 + [jnp.zeros((d, LANES - n_heads), F32)], axis=1).astype(BF16)
        g_mix = norm_mix_g[layer].reshape(1, d)
        cw, cb = ssm_conv_w[layer], ssm_conv_b[layer].reshape(1, conv_dim)
        dtb = pad_heads(ssm_dt_bias[layer])
        a_neg = pad_heads(-jnp.exp(ssm_a_log[layer].astype(F32)))
        dsk = jnp.repeat(ssm_d[layer].astype(F32), SSM_HEAD_DIM).reshape(1, d_inner)
        ng = ssm_norm_g[layer].reshape(1, d_inner)
        qg = jnp.tile(q_norm_g[layer], LANES // QK_DIM).reshape(1, LANES)
        kg = jnp.tile(k_norm_g[layer], LANES // QK_DIM).reshape(1, LANES)
        lams = tuple(t[layer].reshape(1, QK_DIM) for t in (lambda_q1, lambda_k1, lambda_q2, lambda_k2))
        sg = attn_subln_g[layer].reshape(1, V_DIM)
        wso, wao, wo = (t[layer].astype(BF16) for t in (w_ssd_o, w_attn_o, w_o))
        g_ffn = norm_ffn_g[layer].reshape(1, d)
        wup, wdn = w_ffn_up[layer].astype(BF16), w_ffn_down[layer].astype(BF16)
        fcw, fcb = ffn_conv_w[layer], ffn_conv_b[layer].reshape(1, 2 * dff)

        z, xbc, qb, kf, kb, v, vt, gs, ga, dt = _in_proj(xp, g_mix, w_perm, tables_p, qg, kg, bd, widths, tm_p, sp)
        y_ssd, ctail, st = _ssd_prompt(z, xbc, dt, cw, cb, dtb, a_neg, expand, dsk, ng, bp, sp)
        vt = vt.reshape(bp * KV_HEADS * (V_DIM + ONES_ROWS), sp)
        y_att = _attn_prompt(lams, sg.reshape(V_DIM, 1), qb, kb, vt, bp, sp, lam_init, tq)
        x1, hn = _out_proj(y_ssd, y_att, gs, ga, xp, wso, wao, wo, g_ffn, tm_p)
        xp, ftail = _ffn_prompt(hn, x1, wup, wdn, fcw, fcb, bp, sp, tm_f)
        outs[0].append(kf.reshape(bp, sp, KV_HEADS, 2 * QK_DIM))
        outs[1].append(v.reshape(bp, sp, KV_HEADS, V_DIM))
        outs[2].append(ctail[:, SUBLANES - (SSM_CONV - 1):, :])
        outs[3].append(st.reshape(bp, SSM_GROUPS, SSM_STATE, heads_g, SSM_HEAD_DIM).transpose(0, 1, 3, 4, 2))
        outs[4].append(ftail[:, SUBLANES - (FFN_CONV - 1):, :])

        z, xbc, qb, kf, _, v, _, gs, ga, dt = _in_proj(xs_, g_mix, w_perm, tables_s, qg, kg, bd, widths, bs, bs)
        sc = state_conv[layer]
        xs_c, bm, cm, xdt, dec = _ssd_sample_pre(xbc, sc[:, 0], sc[:, 1], sc[:, 2], dt, cw, cb, dtb, a_neg, expand)
        h_new, y_ssd = _ssd_sample_step(state_ssm, layer, xdt, dec, bm, cm, xs_c, z, dsk, ng)
        y_att = _attn_sample(page_table, lams, sg, qb, kf, v, cache_k, cache_v, layer, lam_init)
        x1, hn = _out_proj(y_ssd.reshape(bs, d_inner), y_att.reshape(bs, att_w), gs, ga, xs_, wso, wao, wo, g_ffn, bs)
        sf = state_ffn_conv[layer]
        xs_, u_new = _ffn_sample(hn, x1, sf[:, 0], sf[:, 1], wup, wdn, fcw, fcb)
        outs[5].append(kf.reshape(bs, ss, KV_HEADS, 2 * QK_DIM))
        outs[6].append(v.reshape(bs, ss, KV_HEADS, V_DIM))
        outs[7].append(jnp.concatenate([sc[:, 1:], xbc[:, None, :]], axis=1))
        outs[8].append(h_new)
        outs[9].append(jnp.concatenate([sf[:, 1:], u_new[:, None, :]], axis=1))

    stacked = [jnp.stack(o) for o in outs]
    return (xp.reshape(bp, sp, d), xs_.reshape(bs, ss, d), *stacked)
```

```python
import functools
import math

import jax
import jax.numpy as jnp
import numpy as np
from jax import lax
from jax.experimental import pallas as pl
from jax.experimental.pallas import tpu as pltpu

F32 = jnp.float32
BF16 = jnp.bfloat16
HIGHEST = lax.Precision.HIGHEST

EPS = 1e-6
PAGE_SIZE = 128
SSM_HEAD_DIM = 64
SSM_GROUPS = 2
SSM_STATE = 64
SSM_CONV = 4
SSM_CHUNK = 128
KV_HEADS = 4
KV_REP = 2
QK_DIM = 64
V_DIM = 128
ROT_DIM = QK_DIM // 4
ROPE_THETA = 500000.0
FFN_CONV = 3
QK_SCALE = QK_DIM ** -0.5 * math.log2(math.e)
ONES_ROWS = 16
LANES = 128
SUBLANES = 8
PAGES_PER_STEP = 16
MIB = 1024 * 1024

NT_DIMS = (((1,), (1,)), ((), ()))
TN_DIMS = (((0,), (0,)), ((), ()))


def _params(semantics, vmem_mib):
    return pltpu.CompilerParams(dimension_semantics=semantics, vmem_limit_bytes=vmem_mib * MIB)


def _resident(shape):
    nd = len(shape)
    return pl.BlockSpec(shape, lambda *_: (0,) * nd, pipeline_mode=pl.Buffered(1))


def _sigmoid(x):
    return 1.0 / (1.0 + jnp.exp(-x))


def _silu(x):
    return x * _sigmoid(x)


def _softplus(x):
    return jnp.maximum(x, 0.0) + jnp.log1p(jnp.exp(-jnp.abs(x)))


def _split3(x):
    hi = x.astype(BF16)
    r1 = x - hi.astype(F32)
    mid = r1.astype(BF16)
    lo = (r1 - mid.astype(F32)).astype(BF16)
    return hi, mid, lo


def _select_matmul(x, sel):
    hi, mid, lo = _split3(x)
    return (jnp.dot(hi, sel, preferred_element_type=F32) + jnp.dot(mid, sel, preferred_element_type=F32)
            + jnp.dot(lo, sel, preferred_element_type=F32))


def _rms(x, axis=-1):
    return x * lax.rsqrt(jnp.mean(x * x, axis=axis, keepdims=True) + EPS)


def _inproj_body(x_ref, g_ref, w_ref, cos_ref, sa_ref, sb_ref, qg_ref, kg_ref, bd_ref,
                 z_ref, xbc_ref, qb_ref, kf_ref, kb_ref, v_ref, vt_ref, gs_ref, ga_ref, dt_ref):
    h = (_rms(x_ref[...]) * g_ref[...]).astype(BF16)
    cos, sa, sb, bd = cos_ref[...], sa_ref[...], sb_ref[...], bd_ref[...]
    off = [0]

    def proj(n):
        u = jnp.dot(h, w_ref[:, off[0]:off[0] + n], preferred_element_type=F32)
        off[0] += n
        return u

    def norm_rope(xb, g):
        ms = jnp.dot((xb * xb).astype(BF16), bd, preferred_element_type=F32) * (1.0 / QK_DIM)
        y = xb * lax.rsqrt(ms + EPS) * g
        return y * cos + pltpu.roll(y, LANES - ROT_DIM // 2, 1) * sa + pltpu.roll(y, ROT_DIM // 2, 1) * sb

    z_ref[...] = proj(z_ref.shape[-1])
    xbc_ref[...] = proj(xbc_ref.shape[-1])
    q = proj(qb_ref.shape[-1])
    for j in range(q.shape[-1] // LANES):
        sl = slice(j * LANES, (j + 1) * LANES)
        qb_ref[:, sl] = (norm_rope(q[:, sl], qg_ref[...]) * QK_SCALE).astype(BF16)
    tm = x_ref.shape[0]
    k = proj(kb_ref.shape[-1])
    for hd in range(KV_HEADS):
        sl = slice(hd * LANES, (hd + 1) * LANES)
        kr = norm_rope(k[:, sl], kg_ref[...])
        kf_ref[pl.ds(hd, tm, stride=KV_HEADS), :] = kr
        kb_ref[:, sl] = kr.astype(BF16)
    v = proj(KV_HEADS * V_DIM)
    ones = jnp.ones((ONES_ROWS, tm), BF16)
    for hd in range(KV_HEADS):
        vh = v[:, hd * V_DIM:(hd + 1) * V_DIM]
        v_ref[pl.ds(hd, tm, stride=KV_HEADS), :] = vh
        vt_ref[0, hd, :V_DIM, :] = vh.T.astype(BF16)
        vt_ref[0, hd, V_DIM:, :] = ones
    gs_ref[...] = proj(gs_ref.shape[-1])
    ga_ref[...] = proj(ga_ref.shape[-1])
    dt_ref[...] = proj(dt_ref.shape[-1])


def _in_proj(x, g, w, tables, qg, kg, bd, widths, tm, seq):
    rows, d = x.shape
    nt = seq // tm
    wz, wxbc, wq, wk, wv, wgs, wga, wdt = widths
    row = lambda n: pl.BlockSpec((tm, n), lambda i: (i, 0))
    tab = pl.BlockSpec((tm, LANES), lambda i: (i % nt, 0))
    f32 = lambda n: jax.ShapeDtypeStruct((rows, n), F32)
    b16 = lambda n: jax.ShapeDtypeStruct((rows, n), BF16)
    vt_rows = V_DIM + ONES_ROWS
    assert wk == KV_HEADS * LANES and wv == KV_HEADS * V_DIM and V_DIM == LANES
    head_rows = pl.BlockSpec((tm * KV_HEADS, LANES), lambda i: (i, 0))
    head_rows_shape = jax.ShapeDtypeStruct((rows * KV_HEADS, LANES), F32)
    return pl.pallas_call(
        _inproj_body,
        grid=(rows // tm,),
        in_specs=[row(d), _resident(g.shape), _resident(w.shape), tab, tab, tab,
                  _resident(qg.shape), _resident(kg.shape), _resident(bd.shape)],
        out_specs=[row(wz), row(wxbc), row(wq), head_rows, row(wk), head_rows,
                   pl.BlockSpec((1, KV_HEADS, vt_rows, tm), lambda i: (i // nt, 0, 0, i % nt)),
                   row(wgs), row(wga), row(wdt)],
        out_shape=[f32(wz), f32(wxbc), b16(wq), head_rows_shape, b16(wk), head_rows_shape,
                   jax.ShapeDtypeStruct((rows // seq, KV_HEADS, vt_rows, seq), BF16),
                   f32(wgs), f32(wga), f32(wdt)],
        compiler_params=_params(("parallel",), 52),
        name="in_proj",
    )(x, g, w, *tables, qg, kg, bd)


def _rope_tables(pos):
    half = ROT_DIM // 2
    inv = ROPE_THETA ** (-jnp.arange(half, dtype=F32) * 2.0 / ROT_DIM)
    ang = pos.astype(F32)[:, None] * inv[None, :]
    cos, sin = jnp.cos(ang), jnp.sin(ang)
    n = pos.shape[0]
    rest = QK_DIM - ROT_DIM
    cos_h = jnp.concatenate([cos, cos, jnp.ones((n, rest), F32)], axis=1)
    sa_h = jnp.concatenate([-sin, jnp.zeros((n, half + rest), F32)], axis=1)
    sb_h = jnp.concatenate([jnp.zeros((n, half), F32), sin, jnp.zeros((n, rest), F32)], axis=1)
    rep = LANES // QK_DIM
    return tuple(jnp.tile(t, (1, rep)) for t in (cos_h, sa_h, sb_h))


def _ssd_body(z_ref, xbc_ref, dt_ref, cw_ref, cb_ref, dtb_ref, a_ref, e_ref, dsk_ref, ng_ref,
              y_ref, ctail_ref, st_ref, xpad_scr, h_scr, *, ch, d_inner):
    c = pl.program_id(1)
    nh_g = d_inner // SSM_GROUPS
    pad = SUBLANES

    @pl.when(c == 0)
    def _():
        xpad_scr[0:pad, :] = jnp.zeros((pad, xpad_scr.shape[1]), F32)
        h_scr[...] = jnp.zeros(h_scr.shape, F32)

    xbc = xbc_ref[...]
    xpad_scr[pad:pad + ch, :] = xbc
    cw = cw_ref[...]
    conv = cb_ref[...] + cw[SSM_CONV - 1:SSM_CONV] * xbc
    for kk in range(SSM_CONV - 1):
        start = pad - (SSM_CONV - 1) + kk
        conv = conv + cw[kk:kk + 1] * xpad_scr[start:start + ch, :]
    xpad_scr[0:pad, :] = xpad_scr[ch:ch + pad, :]
    act = _silu(conv)
    xs = act[:, :d_inner]
    bmb = act[:, d_inner:d_inner + LANES].astype(BF16)
    cmb = act[:, d_inner + LANES:d_inner + 2 * LANES].astype(BF16)

    dt = _softplus(dt_ref[...] + dtb_ref[...])
    da = dt * a_ref[...]
    row = lax.broadcasted_iota(jnp.int32, (ch, ch), 0)
    col = lax.broadcasted_iota(jnp.int32, (ch, ch), 1)
    causal = row >= col
    da3 = jnp.concatenate(_split3(da), axis=1)
    cs3 = jnp.dot(causal.astype(BF16), da3, preferred_element_type=F32)
    cs = cs3[:, :LANES] + cs3[:, LANES:2 * LANES] + cs3[:, 2 * LANES:]
    cs_t = cs.T
    e = e_ref[...]
    cs_e = _select_matmul(cs, e)
    dt_e = _select_matmul(dt, e)
    xdt = xs * dt_e
    last = cs_e[ch - 1:ch, :]
    xdtb = xdt.astype(BF16)
    wendb = (xdt * jnp.exp(last - cs_e)).astype(BF16)
    off_scale = jnp.exp(cs_e)
    cdec = jnp.exp(last)

    lane = lax.broadcasted_iota(jnp.int32, (ch, LANES), 1)
    lo = lane < SSM_HEAD_DIM
    zero_b = jnp.zeros((ch, LANES), BF16)
    heads_g = nh_g // SSM_HEAD_DIM
    ys = []
    for g in range(SSM_GROUPS):
        bg = bmb[:, g * SSM_STATE:(g + 1) * SSM_STATE]
        cg = cmb[:, g * SSM_STATE:(g + 1) * SSM_STATE]
        cb = lax.dot_general(cg, bg, NT_DIMS, preferred_element_type=F32)
        hg = h_scr[g]
        yoff = jnp.dot(cg, hg.astype(BF16), preferred_element_type=F32)
        s_new = lax.dot_general(bg, wendb[:, g * nh_g:(g + 1) * nh_g], TN_DIMS, preferred_element_type=F32)
        h_scr[g] = hg * cdec[:, g * nh_g:(g + 1) * nh_g] + s_new
        for pr in range(heads_g // 2):
            r0 = g * heads_g + 2 * pr
            gs = []
            for r in (r0, r0 + 1):
                seg = cs[:, r:r + 1] - cs_t[r:r + 1, :]
                gs.append(jnp.where(causal, cb * jnp.exp(seg), 0.0))
            gc = jnp.concatenate(gs, axis=1).astype(BF16)
            blk = xdtb[:, r0 * SSM_HEAD_DIM:(r0 + 2) * SSM_HEAD_DIM]
            rhs = jnp.concatenate([jnp.where(lo, blk, zero_b), jnp.where(lo, zero_b, blk)], axis=0)
            yd = jnp.dot(gc, rhs, preferred_element_type=F32)
            sl = slice(r0 * SSM_HEAD_DIM, (r0 + 2) * SSM_HEAD_DIM)
            ys.append(yd + yoff[:, 2 * pr * SSM_HEAD_DIM:(2 * pr + 2) * SSM_HEAD_DIM] * off_scale[:, sl])
    y = jnp.concatenate(ys, axis=1) + xs * dsk_ref[...]
    yg = y * _silu(z_ref[...])
    outs = [_rms(yg[:, g * nh_g:(g + 1) * nh_g]) for g in range(SSM_GROUPS)]
    y_ref[...] = (jnp.concatenate(outs, axis=1) * ng_ref[...]).astype(BF16)

    @pl.when(c == pl.num_programs(1) - 1)
    def _():
        ctail_ref[0] = xpad_scr[0:pad, :]
        st_ref[0] = h_scr[...]


def _ssd_prompt(z, xbc, dt, cw, cb, dtb, a, e, dsk, ng, batch, seq):
    ch = SSM_CHUNK
    nc = seq // ch
    d_inner = z.shape[1]
    conv_dim = xbc.shape[1]
    nh_g = d_inner // SSM_GROUPS
    row_spec = lambda n: pl.BlockSpec((ch, n), lambda b, c: (b * nc + c, 0))
    return pl.pallas_call(
        functools.partial(_ssd_body, ch=ch, d_inner=d_inner),
        grid=(batch, nc),
        in_specs=[row_spec(d_inner), row_spec(conv_dim), row_spec(LANES)]
        + [_resident(t.shape) for t in (cw, cb, dtb, a, e, dsk, ng)],
        out_specs=[row_spec(d_inner),
                   pl.BlockSpec((1, SUBLANES, conv_dim), lambda b, c: (b, 0, 0)),
                   pl.BlockSpec((1, SSM_GROUPS, SSM_STATE, nh_g), lambda b, c: (b, 0, 0, 0))],
        out_shape=[jax.ShapeDtypeStruct((batch * seq, d_inner), BF16),
                   jax.ShapeDtypeStruct((batch, SUBLANES, conv_dim), F32),
                   jax.ShapeDtypeStruct((batch, SSM_GROUPS, SSM_STATE, nh_g), F32)],
        scratch_shapes=[pltpu.VMEM((SUBLANES + ch, conv_dim), F32),
                        pltpu.VMEM((SSM_GROUPS, SSM_STATE, nh_g), F32)],
        compiler_params=_params(("parallel", "arbitrary"), 40),
        name="ssd_prompt",
    )(z, xbc, dt, cw, cb, dtb, a, e, dsk, ng)


def _lambda(lq1_ref, lk1_ref, lq2_ref, lk2_ref, lam_init):
    s1 = jnp.sum(lq1_ref[...] * lk1_ref[...], axis=-1, keepdims=True)
    s2 = jnp.sum(lq2_ref[...] * lk2_ref[...], axis=-1, keepdims=True)
    return jnp.exp(s1) - jnp.exp(s2) + lam_init


def _attn_body(lq1_ref, lk1_ref, lq2_ref, lk2_ref, sg_ref, q_ref, k_ref, vt_ref, o_ref,
               qp_scr, m_scr, acc_scr, sa_scr, sb_scr, *, tq, nq, lam_init):
    def start_block(qi):
        q = q_ref[pl.ds(pl.multiple_of(qi * tq, tq), tq), :]
        lane = lax.broadcasted_iota(jnp.int32, (tq, LANES), 1)
        lo = lane < QK_DIM
        zero_b = jnp.zeros((tq, LANES), BF16)
        for r in range(KV_REP):
            qr = q[:, r * LANES:(r + 1) * LANES]
            qp_scr[(2 * r) * tq:(2 * r + 1) * tq, :] = jnp.where(lo, qr, zero_b)
            qp_scr[(2 * r + 1) * tq:(2 * r + 2) * tq, :] = jnp.where(lo, zero_b, qr)
        m_scr[...] = jnp.full(m_scr.shape, -jnp.inf, F32)
        acc_scr[...] = jnp.zeros(acc_scr.shape, F32)
        scores(0, sa_scr)

    def scores(j, s_scr):
        start = pl.multiple_of(j * tq, tq)
        s_scr[...] = lax.dot_general(k_ref[pl.ds(start, tq), :], qp_scr[...], NT_DIMS,
                                     preferred_element_type=F32)

    def consume(j, s_scr, diagonal):
        start = pl.multiple_of(j * tq, tq)
        vtb = vt_ref[:, pl.ds(start, tq)]
        m_prev = m_scr[...]
        ps, ms = [], []
        for c in range(2 * KV_REP * tq // LANES):
            cols = slice(c * LANES, (c + 1) * LANES)
            s = s_scr[:, cols]
            if diagonal:
                key = lax.broadcasted_iota(jnp.int32, (tq, LANES), 0)
                qry = lax.broadcasted_iota(jnp.int32, (tq, LANES), 1) + (c * LANES) % tq
                s = jnp.where(key <= qry, s, -jnp.inf)
            m_c = jnp.maximum(m_prev[:, cols], jnp.max(s, axis=0, keepdims=True))
            ps.append(jnp.exp2(s - m_c).astype(BF16))
            ms.append(m_c)
        m_new = jnp.concatenate(ms, axis=1)
        alpha = jnp.exp2(m_prev - m_new)
        pv = jnp.dot(vtb, jnp.concatenate(ps, axis=1), preferred_element_type=F32)
        acc_scr[...] = alpha * acc_scr[...] + pv
        m_scr[...] = m_new

    def pair(i, carry):
        scores(2 * i + 1, sb_scr)
        consume(2 * i, sa_scr, False)
        scores(2 * i + 2, sa_scr)
        consume(2 * i + 1, sb_scr, False)
        return carry

    lam = _lambda(lq1_ref, lk1_ref, lq2_ref, lk2_ref, lam_init)

    def query_block(qi, carry):
        lax.fori_loop(0, qi // 2, pair, 0)

        @pl.when(qi % 2 == 1)
        def _():
            scores(qi, sb_scr)
            consume(qi - 1, sa_scr, False)
            consume(qi, sb_scr, True)

        @pl.when(qi % 2 == 0)
        def _():
            consume(qi, sa_scr, True)

        rows = pl.ds(pl.multiple_of(qi * tq, tq), tq)
        for r in range(KV_REP):
            c1 = slice((2 * r) * tq, (2 * r + 1) * tq)
            c2 = slice((2 * r + 1) * tq, (2 * r + 2) * tq)
            o = (acc_scr[:V_DIM, c1] / acc_scr[V_DIM:V_DIM + 1, c1]
                 - lam * (acc_scr[:V_DIM, c2] / acc_scr[V_DIM:V_DIM + 1, c2]))
            y = _rms(o, axis=0) * sg_ref[...] * (1.0 - lam_init)
            o_ref[rows, r * LANES:(r + 1) * LANES] = y.T.astype(BF16)
        start_block(jnp.minimum(qi + 1, nq - 1))
        return carry

    start_block(0)
    lax.fori_loop(0, nq, query_block, 0)


def _attn_prompt(lams, sg_col, qb, kb, vt, batch, seq, lam_init, tq):
    nq = seq // tq
    cols = 2 * KV_REP * tq
    att_w = qb.shape[1]
    return pl.pallas_call(
        functools.partial(_attn_body, tq=tq, nq=nq, lam_init=lam_init),
        grid=(batch, KV_HEADS),
        in_specs=[_resident(t.shape) for t in lams] + [_resident(sg_col.shape)] + [
            pl.BlockSpec((seq, KV_REP * LANES), lambda b, h: (b, h)),
            pl.BlockSpec((seq, LANES), lambda b, h: (b, h)),
            pl.BlockSpec((V_DIM + ONES_ROWS, seq), lambda b, h: (b * KV_HEADS + h, 0))],
        out_specs=pl.BlockSpec((seq, KV_REP * LANES), lambda b, h: (b, h)),
        out_shape=jax.ShapeDtypeStruct((batch * seq, att_w), BF16),
        scratch_shapes=[pltpu.VMEM((cols, LANES), BF16), pltpu.VMEM((1, cols), F32),
                        pltpu.VMEM((V_DIM + ONES_ROWS, cols), F32), pltpu.VMEM((tq, cols), F32),
                        pltpu.VMEM((tq, cols), F32)],
        compiler_params=_params(("parallel", "parallel"), 40),
        name="attn_prompt",
    )(*lams, sg_col, qb, kb, vt)


def _ssd_pre_body(xbc_ref, s0_ref, s1_ref, s2_ref, dt_ref, cw_ref, cb_ref, dtb_ref, a_ref, e_ref,
                  xs_ref, bm_ref, cm_ref, xdt_ref, dec_ref, *, d_inner):
    cw = cw_ref[...]
    conv = (cb_ref[...] + cw[0:1] * s0_ref[...] + cw[1:2] * s1_ref[...] + cw[2:3] * s2_ref[...]
            + cw[3:4] * xbc_ref[...])
    act = _silu(conv)
    xs = act[:, :d_inner]
    dt = _softplus(dt_ref[...] + dtb_ref[...])
    e = e_ref[...]
    dt_e = _select_matmul(dt, e)
    dec_e = _select_matmul(jnp.exp(dt * a_ref[...]), e)
    xs_ref[...] = xs
    bm_ref[...] = act[:, d_inner:d_inner + LANES]
    cm_ref[...] = act[:, d_inner + LANES:d_inner + 2 * LANES]
    xdt_ref[...] = xs * dt_e
    dec_ref[...] = dec_e


def _ssd_sample_pre(xbc, s0, s1, s2, dt, cw, cb, dtb, a, e):
    n, d_inner = xbc.shape[0], e.shape[1]
    wide = jax.ShapeDtypeStruct((n, d_inner), F32)
    narrow = jax.ShapeDtypeStruct((n, LANES), F32)
    return pl.pallas_call(
        functools.partial(_ssd_pre_body, d_inner=d_inner),
        out_shape=[wide, narrow, narrow, wide, wide],
        compiler_params=pltpu.CompilerParams(vmem_limit_bytes=32 * MIB),
        name="ssd_sample_pre",
    )(xbc, s0, s1, s2, dt, cw, cb, dtb, a, e)


def _ssd_step_body(h0_ref, xdt_ref, dec_ref, bm_ref, cm_ref, xs_ref, z_ref, dsk_ref, ng_ref,
                   hn_ref, y_ref, *, nh_g):
    xdt, dec, bm, cm = xdt_ref[0], dec_ref[0], bm_ref[0], cm_ref[0]
    krows = 2 * SUBLANES

    def tile_rows(rows):
        w = rows[0].shape[1]
        return jnp.concatenate([r.astype(BF16) for r in rows] + [jnp.zeros((krows - len(rows), w), BF16)], axis=0)

    ones3 = tile_rows([jnp.ones((1, SSM_STATE), F32)] * 3)
    ys = []
    for g in range(SSM_GROUPS):
        h0 = h0_ref[0, 0, g].reshape(nh_g, SSM_STATE)
        bg = bm[:, g * SSM_STATE:(g + 1) * SSM_STATE]
        cg = cm[:, g * SSM_STATE:(g + 1) * SSM_STATE]
        xg = xdt[:, g * nh_g:(g + 1) * nh_g]
        dg = dec[:, g * nh_g:(g + 1) * nh_g]
        outer = lax.dot_general(tile_rows([xg]), tile_rows([bg]), TN_DIMS, preferred_element_type=F32)
        dcol = lax.dot_general(tile_rows(list(_split3(dg))), ones3, TN_DIMS, preferred_element_type=F32)
        hn_ref[0, g] = (h0 * dcol + outer).reshape(hn_ref.shape[2:])
        ch0 = lax.dot_general(tile_rows([cg]), h0.astype(BF16), NT_DIMS, preferred_element_type=F32)[0:1]
        ys.append(dg * ch0 + xg * jnp.sum(cg * bg, axis=-1, keepdims=True))
    y = jnp.concatenate(ys, axis=1) + xs_ref[0] * dsk_ref[...]
    yg = y * _silu(z_ref[0])
    outs = [_rms(yg[:, g * nh_g:(g + 1) * nh_g]) for g in range(SSM_GROUPS)]
    y_ref[0] = (jnp.concatenate(outs, axis=1) * ng_ref[...]).astype(BF16)


def _ssd_sample_step(state, layer, xdt, dec, bm, cm, xs, z, dsk, ng):
    n = state.shape[1]
    st_dims = state.shape[2:]
    d_inner = xs.shape[-1]
    row3 = lambda w: pl.BlockSpec((1, 1, w), lambda b: (b, 0, 0))
    r3 = lambda t: t.reshape(n, 1, t.shape[-1])
    return pl.pallas_call(
        functools.partial(_ssd_step_body, nh_g=d_inner // SSM_GROUPS),
        grid=(n,),
        in_specs=[pl.BlockSpec((1, 1) + st_dims, lambda b: (layer, b, 0, 0, 0, 0)),
                  row3(d_inner), row3(d_inner), row3(LANES), row3(LANES), row3(d_inner), row3(d_inner),
                  _resident(dsk.shape), _resident(ng.shape)],
        out_specs=[pl.BlockSpec((1,) + st_dims, lambda b: (b, 0, 0, 0, 0)), row3(d_inner)],
        out_shape=[jax.ShapeDtypeStruct((n,) + st_dims, F32), jax.ShapeDtypeStruct((n, 1, d_inner), BF16)],
        compiler_params=_params(("parallel",), 32),
        name="ssd_sample_step",
    )(state, r3(xdt), r3(dec), r3(bm), r3(cm), r3(xs), r3(z), dsk, ng)


def _attn_sample_body(pt_ref, lq1_ref, lk1_ref, lq2_ref, lk2_ref, sg_ref, q_ref, kn_ref, vn_ref, *rest,
                      pages, lam_init):
    k_refs, v_refs = rest[:pages], rest[pages:2 * pages]
    o_ref, qp_scr, bias_scr, m_scr, l_scr, acc_scr = rest[2 * pages:]
    j = pl.program_id(1)
    prow = PAGE_SIZE * KV_HEADS

    def head_rows(row128):
        per_head = [jnp.broadcast_to(row128[:, h * LANES:(h + 1) * LANES], (2, LANES)) for h in range(KV_HEADS)]
        return jnp.concatenate(per_head * KV_REP, axis=0)

    @pl.when(j == 0)
    def _():
        q = q_ref[0].astype(F32)
        lane = lax.broadcasted_iota(jnp.int32, (2, LANES), 1)
        comp = lax.broadcasted_iota(jnp.int32, (2, LANES), 0)
        own_half = (lane // QK_DIM) == comp
        rows = []
        for r in range(KV_REP):
            for h in range(KV_HEADS):
                qhr = q[:, (KV_REP * h + r) * LANES:(KV_REP * h + r + 1) * LANES]
                rows.append(jnp.where(own_half, jnp.broadcast_to(qhr, (2, LANES)), 0.0))
        qp_scr[...] = jnp.concatenate(rows, axis=0).astype(BF16)
        row_head = (lax.broadcasted_iota(jnp.int32, bias_scr.shape, 0) // 2) % KV_HEADS
        col_head = lax.broadcasted_iota(jnp.int32, bias_scr.shape, 1) % KV_HEADS
        bias_scr[...] = jnp.where(row_head == col_head, 0.0, -jnp.inf)
        m_scr[...] = jnp.full(m_scr.shape, -jnp.inf, F32)
        l_scr[...] = jnp.zeros(l_scr.shape, F32)
        acc_scr[...] = jnp.zeros(acc_scr.shape, F32)

    qp = qp_scr[...]
    s = jnp.concatenate(
        [lax.dot_general(qp, k_refs[i][...].astype(BF16), NT_DIMS, preferred_element_type=F32)
         for i in range(pages)], axis=1) + bias_scr[...]
    m_prev = m_scr[...]
    m_new = jnp.maximum(m_prev, jnp.max(s, axis=-1, keepdims=True))
    alpha = jnp.exp2(m_prev - m_new)
    p = jnp.exp2(s - m_new)
    l_scr[...] = alpha * l_scr[...] + jnp.sum(p, axis=-1, keepdims=True)
    pb = p.astype(BF16)
    pv = jnp.dot(pb[:, 0:prow], v_refs[0][...].astype(BF16), preferred_element_type=F32)
    for i in range(1, pages):
        pv = pv + jnp.dot(pb[:, i * prow:(i + 1) * prow], v_refs[i][...].astype(BF16),
                          preferred_element_type=F32)
    acc_scr[...] = alpha * acc_scr[...] + pv
    m_scr[...] = m_new

    @pl.when(j == pl.num_programs(1) - 1)
    def _():
        k_self = head_rows(kn_ref[0]).astype(BF16).astype(F32)
        s_self = jnp.sum(qp.astype(F32) * k_self, axis=-1, keepdims=True)
        m_prev = m_scr[...]
        m_fin = jnp.maximum(m_prev, s_self)
        alpha = jnp.exp2(m_prev - m_fin)
        p_self = jnp.exp2(s_self - m_fin)
        l_fin = alpha * l_scr[...] + p_self
        acc = (alpha * acc_scr[...] + p_self * head_rows(vn_ref[0])) / l_fin
        lam = _lambda(lq1_ref, lk1_ref, lq2_ref, lk2_ref, lam_init)
        for h in range(KV_HEADS):
            for r in range(KV_REP):
                i1 = (r * KV_HEADS + h) * 2
                o = acc[i1:i1 + 1, :] - lam * acc[i1 + 1:i1 + 2, :]
                y = _rms(o) * sg_ref[...] * (1.0 - lam_init)
                dst = (KV_REP * h + r) * LANES
                o_ref[0, :, dst:dst + LANES] = y.astype(BF16)


def _attn_sample(page_table, lams, sg, qb, k_new, v_new, cache_k, cache_v, layer, lam_init):
    n, n_pages = page_table.shape
    pages = PAGES_PER_STEP
    while n_pages % pages:
        pages //= 2
    att_w = qb.shape[-1]
    kw = KV_HEADS * LANES
    nrow = 2 * KV_REP * KV_HEADS
    n_pool = cache_k.shape[1]
    prow = PAGE_SIZE * KV_HEADS
    ck = cache_k.reshape(cache_k.shape[0] * n_pool * prow, 2 * QK_DIM)
    cv = cache_v.reshape(cache_v.shape[0] * n_pool * prow, V_DIM)
    page_table = page_table + layer * n_pool
    const = lambda shape: pl.BlockSpec(shape, lambda b, j, pt: (0,) * len(shape))
    row3 = lambda w: pl.BlockSpec((1, 1, w), lambda b, j, pt: (b, 0, 0))

    def page_spec(i):
        return pl.BlockSpec((prow, LANES), lambda b, j, pt: (pt[b * n_pages + j * pages + i], 0))

    grid_spec = pltpu.PrefetchScalarGridSpec(
        num_scalar_prefetch=1,
        grid=(n, n_pages // pages),
        in_specs=[const(t.shape) for t in lams] + [const(sg.shape), row3(att_w), row3(kw), row3(kw)]
        + [page_spec(i) for i in range(pages)] + [page_spec(i) for i in range(pages)],
        out_specs=row3(att_w),
        scratch_shapes=[pltpu.VMEM((nrow, LANES), BF16), pltpu.VMEM((nrow, pages * prow), F32),
                        pltpu.VMEM((nrow, 1), F32), pltpu.VMEM((nrow, 1), F32), pltpu.VMEM((nrow, LANES), F32)],
    )
    return pl.pallas_call(
        functools.partial(_attn_sample_body, pages=pages, lam_init=lam_init),
        grid_spec=grid_spec,
        out_shape=jax.ShapeDtypeStruct((n, 1, att_w), BF16),
        compiler_params=_params(("parallel", "arbitrary"), 40),
        name="attn_sample",
    )(page_table.reshape(-1), *lams, sg, qb.reshape(n, 1, att_w), k_new.reshape(n, 1, kw),
      v_new.reshape(n, 1, kw), *([ck] * pages), *([cv] * pages))


def _merge_residual_norm(ys_ref, ya_ref, gs_ref, ga_ref, x_ref, wso_ref, wao_ref, wo_ref, g_ref):
    o_ssd = jnp.dot(ys_ref[...], wso_ref[...], preferred_element_type=F32)
    o_att = jnp.dot(ya_ref[...], wao_ref[...], preferred_element_type=F32)
    merged = _sigmoid(gs_ref[...]) * o_ssd + _sigmoid(ga_ref[...]) * o_att
    x1 = x_ref[...] + jnp.dot(merged.astype(BF16), wo_ref[...], preferred_element_type=F32)
    return x1, (_rms(x1) * g_ref[...]).astype(BF16)


def _outproj_body(ys_ref, ya_ref, gs_ref, ga_ref, x_ref, wso_ref, wao_ref, wo_ref, g_ref, x1_ref, hn_ref):
    x1, hn = _merge_residual_norm(ys_ref, ya_ref, gs_ref, ga_ref, x_ref, wso_ref, wao_ref, wo_ref, g_ref)
    x1_ref[...] = x1
    hn_ref[...] = hn


def _out_proj(ys, ya, gs, ga, x, wso, wao, wo, g, tm):
    rows, d = x.shape
    row_spec = lambda n: pl.BlockSpec((tm, n), lambda i: (i, 0))
    return pl.pallas_call(
        _outproj_body,
        grid=(rows // tm,),
        in_specs=[row_spec(ys.shape[1]), row_spec(ya.shape[1]), row_spec(d), row_spec(d), row_spec(d)]
        + [_resident(t.shape) for t in (wso, wao, wo, g)],
        out_specs=[row_spec(d), row_spec(d)],
        out_shape=[jax.ShapeDtypeStruct((rows, d), F32), jax.ShapeDtypeStruct((rows, d), BF16)],
        compiler_params=_params(("parallel",), 48),
        name="out_proj",
    )(ys, ya, gs, ga, x, wso, wao, wo, g)


def _ffn_chunks(dff):
    tf = 2 * LANES
    assert dff % tf == 0
    return tf, dff // tf


def _mixer_ffn_body(ys_ref, ya_ref, gs_ref, ga_ref, x_ref, wso_ref, wao_ref, wo_ref, g_ref,
                    wup_ref, wdn_ref, cw_ref, cb_ref, y_ref, tail_ref, ubuf_scr, act_scr, *, tm, dff):
    t = pl.program_id(1)
    pad = SUBLANES

    @pl.when(t == 0)
    def _():
        ubuf_scr[0:pad, :] = jnp.zeros((pad, ubuf_scr.shape[1]), F32)

    x1, hn = _merge_residual_norm(ys_ref, ya_ref, gs_ref, ga_ref, x_ref, wso_ref, wao_ref, wo_ref, g_ref)
    tf, nchunk = _ffn_chunks(dff)
    for c in range(nchunk):
        halves = []
        for half in range(2):
            sl = slice(half * dff + c * tf, half * dff + (c + 1) * tf)
            u = jnp.dot(hn, wup_ref[:, sl], preferred_element_type=F32)
            ubuf_scr[pad:pad + tm, sl] = u
            cw = cw_ref[:, sl]
            conv = cb_ref[:, sl] + cw[FFN_CONV - 1:FFN_CONV] * u
            for kk in range(FFN_CONV - 1):
                start = pad - (FFN_CONV - 1) + kk
                conv = conv + cw[kk:kk + 1] * ubuf_scr[start:start + tm, sl]
            ubuf_scr[0:pad, sl] = ubuf_scr[tm:tm + pad, sl]
            halves.append(conv)
        act_scr[:, c * tf:(c + 1) * tf] = (_silu(halves[1]) * halves[0]).astype(BF16)
    y_ref[...] = x1 + jnp.dot(act_scr[...], wdn_ref[...], preferred_element_type=F32)

    @pl.when(t == pl.num_programs(1) - 1)
    def _():
        tail_ref[0] = ubuf_scr[0:pad, :]


def _mixer_ffn_prompt(ys, ya, gs, ga, x, wso, wao, wo, g, wup, wdn, cw, cb, batch, seq, tm):
    d = x.shape[1]
    dff = wdn.shape[0]
    nt = seq // tm
    row_spec = lambda n: pl.BlockSpec((tm, n), lambda b, t: (b * nt + t, 0))
    return pl.pallas_call(
        functools.partial(_mixer_ffn_body, tm=tm, dff=dff),
        grid=(batch, nt),
        in_specs=[row_spec(ys.shape[1]), row_spec(ya.shape[1]), row_spec(d), row_spec(d), row_spec(d)]
        + [_resident(t.shape) for t in (wso, wao, wo, g, wup, wdn, cw, cb)],
        out_specs=[row_spec(d), pl.BlockSpec((1, SUBLANES, 2 * dff), lambda b, t: (b, 0, 0))],
        out_shape=[jax.ShapeDtypeStruct(x.shape, F32), jax.ShapeDtypeStruct((batch, SUBLANES, 2 * dff), F32)],
        scratch_shapes=[pltpu.VMEM((SUBLANES + tm, 2 * dff), F32), pltpu.VMEM((tm, dff), BF16)],
        compiler_params=_params(("parallel", "arbitrary"), 56),
        name="mixer_ffn_prompt",
    )(ys, ya, gs, ga, x, wso, wao, wo, g, wup, wdn, cw, cb)


def _ffn_sample_body(hn_ref, x1_ref, s0_ref, s1_ref, wup_ref, wdn_ref, cw_ref, cb_ref, y_ref, u_ref, *, dff):
    hn = hn_ref[...]
    tf, nchunk = _ffn_chunks(dff)
    acc = jnp.zeros(y_ref.shape, F32)
    for c in range(nchunk):
        halves = []
        for half in range(2):
            sl = slice(half * dff + c * tf, half * dff + (c + 1) * tf)
            u = jnp.dot(hn, wup_ref[:, sl], preferred_element_type=F32)
            u_ref[:, sl] = u
            cw = cw_ref[:, sl]
            halves.append(cb_ref[:, sl] + cw[0:1] * s0_ref[:, sl] + cw[1:2] * s1_ref[:, sl] + cw[2:3] * u)
        act = (_silu(halves[1]) * halves[0]).astype(BF16)
        acc = acc + jnp.dot(act, wdn_ref[c * tf:(c + 1) * tf, :], preferred_element_type=F32)
    y_ref[...] = x1_ref[...] + acc


def _ffn_sample(hn, x1, s0, s1, wup, wdn, cw, cb):
    dff = wdn.shape[0]
    return pl.pallas_call(
        functools.partial(_ffn_sample_body, dff=dff),
        out_shape=[jax.ShapeDtypeStruct(x1.shape, F32), jax.ShapeDtypeStruct((x1.shape[0], 2 * dff), F32)],
        compiler_params=pltpu.CompilerParams(vmem_limit_bytes=52 * MIB),
        name="ffn_sample",
    )(hn, x1, s0, s1, wup, wdn, cw, cb)


def _largest_tile(n, cap):
    t = cap
    while n % t:
        t //= 2
    return t


def kernel(x_prompt, x_sample, cache_k, cache_v, state_conv, state_ssm, state_ffn_conv, page_table, norm_mix_g, w_in, ssm_conv_w, ssm_conv_b, ssm_dt_bias, ssm_a_log, ssm_d, ssm_norm_g, q_norm_g, k_norm_g, lambda_q1, lambda_k1, lambda_q2, lambda_k2, attn_subln_g, w_ssd_o, w_attn_o, w_o, norm_ffn_g, w_ffn_up, ffn_conv_w, ffn_conv_b, w_ffn_down):
    bp, sp, d = x_prompt.shape
    bs, ss, _ = x_sample.shape
    assert ss == 1, "the sample group decodes one token per sequence"
    depth = w_in.shape[0]
    past = page_table.shape[1] * PAGE_SIZE
    d_inner = w_ssd_o.shape[1]
    n_heads = ssm_dt_bias.shape[1]
    conv_dim = ssm_conv_w.shape[2]
    att_w = w_attn_o.shape[1]
    kw = KV_HEADS * 2 * QK_DIM
    vw = KV_HEADS * V_DIM
    dff = w_ffn_down.shape[1]
    heads_g = n_heads // SSM_GROUPS
    nh_g = d_inner // SSM_GROUPS

    sizes = (d_inner, conv_dim, n_heads, att_w, kw, vw, d, d)
    cuts = np.concatenate([[0], np.cumsum(sizes)])
    order = (0, 1, 3, 4, 5, 6, 7, 2)
    widths = tuple(sizes[i] for i in order[:-1]) + (LANES,)

    tables_p = _rope_tables(jnp.arange(sp))
    tables_s = _rope_tables(jnp.broadcast_to(past + jnp.arange(ss), (bs,)))
    bd = jnp.asarray(np.kron(np.eye(LANES // QK_DIM), np.ones((QK_DIM, QK_DIM))), BF16)
    expand = jnp.asarray(np.pad(np.kron(np.eye(n_heads), np.ones((1, SSM_HEAD_DIM))),
                                ((0, LANES - n_heads), (0, 0))), BF16)
    pad_heads = lambda v: jnp.pad(v.astype(F32), (0, LANES - n_heads)).reshape(1, LANES)

    xp = x_prompt.reshape(bp * sp, d)
    xs_ = x_sample.reshape(bs * ss, d)
    tm_p = _largest_tile(bp * sp, 256)
    tq = _largest_tile(sp, 256)
    tm_f = _largest_tile(sp, 256)

    outs = [[] for _ in range(10)]
    for layer in range(depth):
        lam_init = 0.8 - 0.6 * math.exp(-0.3 * layer)
        wl = w_in[layer]
        w_perm = jnp.concatenate([wl[:, cuts[i]:cuts[i + 1]] for i in order]
                                 + [jnp.zeros((d, LANES - n_heads), F32)], axis=1).astype(BF16)
        g_mix = norm_mix_g[layer].reshape(1, d)
        cw, cb = ssm_conv_w[layer], ssm_conv_b[layer].reshape(1, conv_dim)
        dtb = pad_heads(ssm_dt_bias[layer])
        a_neg = pad_heads(-jnp.exp(ssm_a_log[layer].astype(F32)))
        dsk = jnp.repeat(ssm_d[layer].astype(F32), SSM_HEAD_DIM).reshape(1, d_inner)
        ng = ssm_norm_g[layer].reshape(1, d_inner)
        qg = jnp.tile(q_norm_g[layer], LANES // QK_DIM).reshape(1, LANES)
        kg = jnp.tile(k_norm_g[layer], LANES // QK_DIM).reshape(1, LANES)
        lams = tuple(t[layer].reshape(1, QK_DIM) for t in (lambda_q1, lambda_k1, lambda_q2, lambda_k2))
        sg = attn_subln_g[layer].reshape(1, V_DIM)
        wso, wao, wo = (t[layer].astype(BF16) for t in (w_ssd_o, w_attn_o, w_o))
        g_ffn = norm_ffn_g[layer].reshape(1, d)
        wup, wdn = w_ffn_up[layer].astype(BF16), w_ffn_down[layer].astype(BF16)
        fcw, fcb = ffn_conv_w[layer], ffn_conv_b[layer].reshape(1, 2 * dff)

        z, xbc, qb, kf, kb, v, vt, gs, ga, dt = _in_proj(xp, g_mix, w_perm, tables_p, qg, kg, bd, widths, tm_p, sp)
        y_ssd, ctail, st = _ssd_prompt(z, xbc, dt, cw, cb, dtb, a_neg, expand, dsk, ng, bp, sp)
        vt = vt.reshape(bp * KV_HEADS * (V_DIM + ONES_ROWS), sp)
        y_att = _attn_prompt(lams, sg.reshape(V_DIM, 1), qb, kb, vt, bp, sp, lam_init, tq)
        xp, ftail = _mixer_ffn_prompt(y_ssd, y_att, gs, ga, xp, wso, wao, wo, g_ffn, wup, wdn, fcw, fcb,
                                      bp, sp, tm_f)
        outs[0].append(kf.reshape(bp, sp, KV_HEADS, 2 * QK_DIM))
        outs[1].append(v.reshape(bp, sp, KV_HEADS, V_DIM))
        outs[2].append(ctail[:, SUBLANES - (SSM_CONV - 1):, :])
        outs[3].append(st.reshape(bp, SSM_GROUPS, SSM_STATE, heads_g, SSM_HEAD_DIM).transpose(0, 1, 3, 4, 2))
        outs[4].append(ftail[:, SUBLANES - (FFN_CONV - 1):, :])

        z, xbc, qb, kf, _, v, _, gs, ga, dt = _in_proj(xs_, g_mix, w_perm, tables_s, qg, kg, bd, widths, bs, bs)
        sc = state_conv[layer]
        xs_c, bm, cm, xdt, dec = _ssd_sample_pre(xbc, sc[:, 0], sc[:, 1], sc[:, 2], dt, cw, cb, dtb, a_neg, expand)
        h_new, y_ssd = _ssd_sample_step(state_ssm, layer, xdt, dec, bm, cm, xs_c, z, dsk, ng)
        y_att = _attn_sample(page_table, lams, sg, qb, kf, v, cache_k, cache_v, layer, lam_init)
        x1, hn = _out_proj(y_ssd.reshape(bs, d_inner), y_att.reshape(bs, att_w), gs, ga, xs_, wso, wao, wo, g_ffn, bs)
        sf = state_ffn_conv[layer]
        xs_, u_new = _ffn_sample(hn, x1, sf[:, 0], sf[:, 1], wup, wdn, fcw, fcb)
        outs[5].append(kf.reshape(bs, ss, KV_HEADS, 2 * QK_DIM))
        outs[6].append(v.reshape(bs, ss, KV_HEADS, V_DIM))
        outs[7].append(jnp.concatenate([sc[:, 1:], xbc[:, None, :]], axis=1))
        outs[8].append(h_new)
        outs[9].append(jnp.concatenate([sf[:, 1:], u_new[:, None, :]], axis=1))

    stacked = [jnp.stack(o) for o in outs]
    return (xp.reshape(bp, sp, d), xs_.reshape(bs, ss, d), *stacked)
```

```python
import functools
import math

import jax
import jax.numpy as jnp
import numpy as np
from jax import lax
from jax.experimental import pallas as pl
from jax.experimental.pallas import tpu as pltpu

F32 = jnp.float32
BF16 = jnp.bfloat16
HIGHEST = lax.Precision.HIGHEST

EPS = 1e-6
PAGE_SIZE = 128
SSM_HEAD_DIM = 64
SSM_GROUPS = 2
SSM_STATE = 64
SSM_CONV = 4
SSM_CHUNK = 128
KV_HEADS = 4
KV_REP = 2
QK_DIM = 64
V_DIM = 128
ROT_DIM = QK_DIM // 4
ROPE_THETA = 500000.0
FFN_CONV = 3
QK_SCALE = QK_DIM ** -0.5 * math.log2(math.e)
ONES_ROWS = 16
LANES = 128
SUBLANES = 8
PAGES_PER_STEP = 16
MIB = 1024 * 1024

NT_DIMS = (((1,), (1,)), ((), ()))
TN_DIMS = (((0,), (0,)), ((), ()))


def _params(semantics, vmem_mib):
    return pltpu.CompilerParams(dimension_semantics=semantics, vmem_limit_bytes=vmem_mib * MIB)


def _resident(shape):
    nd = len(shape)
    return pl.BlockSpec(shape, lambda *_: (0,) * nd, pipeline_mode=pl.Buffered(1))


def _sigmoid(x):
    return 1.0 / (1.0 + jnp.exp(-x))


def _silu(x):
    h = 0.5 * x
    return h + h * jnp.tanh(h)


def _softplus(x):
    return jnp.maximum(x, 0.0) + jnp.log1p(jnp.exp(-jnp.abs(x)))


def _split3(x):
    hi = x.astype(BF16)
    r1 = x - hi.astype(F32)
    mid = r1.astype(BF16)
    lo = (r1 - mid.astype(F32)).astype(BF16)
    return hi, mid, lo


def _select_matmul(x, sel3):
    return jnp.dot(jnp.concatenate(_split3(x), axis=1), sel3, preferred_element_type=F32)


def _rms(x, axis=-1):
    return x * lax.rsqrt(jnp.mean(x * x, axis=axis, keepdims=True) + EPS)


def _inproj_body(x_ref, g_ref, w_ref, cos_ref, sa_ref, sb_ref, qg_ref, kg_ref, bd_ref,
                 z_ref, xbc_ref, qb_ref, kf_ref, kb_ref, v_ref, vt_ref, gs_ref, ga_ref, dt_ref):
    h = (_rms(x_ref[...]) * g_ref[...]).astype(BF16)
    cos, sa, sb, bd = cos_ref[...], sa_ref[...], sb_ref[...], bd_ref[...]
    off = [0]

    def proj(n):
        u = jnp.dot(h, w_ref[:, off[0]:off[0] + n], preferred_element_type=F32)
        off[0] += n
        return u

    def norm_rope(xb, g):
        ms = jnp.dot((xb * xb).astype(BF16), bd, preferred_element_type=F32) * (1.0 / QK_DIM)
        y = xb * lax.rsqrt(ms + EPS) * g
        return y * cos + pltpu.roll(y, LANES - ROT_DIM // 2, 1) * sa + pltpu.roll(y, ROT_DIM // 2, 1) * sb

    z_ref[...] = proj(z_ref.shape[-1])
    xbc_ref[...] = proj(xbc_ref.shape[-1])
    q = proj(qb_ref.shape[-1])
    for j in range(q.shape[-1] // LANES):
        sl = slice(j * LANES, (j + 1) * LANES)
        qb_ref[:, sl] = (norm_rope(q[:, sl], qg_ref[...]) * QK_SCALE).astype(BF16)
    tm = x_ref.shape[0]
    k = proj(kb_ref.shape[-1])
    for hd in range(KV_HEADS):
        sl = slice(hd * LANES, (hd + 1) * LANES)
        kr = norm_rope(k[:, sl], kg_ref[...])
        kf_ref[pl.ds(hd, tm, stride=KV_HEADS), :] = kr
        kb_ref[:, sl] = kr.astype(BF16)
    v = proj(KV_HEADS * V_DIM)
    ones = jnp.ones((ONES_ROWS, tm), BF16)
    for hd in range(KV_HEADS):
        vh = v[:, hd * V_DIM:(hd + 1) * V_DIM]
        v_ref[pl.ds(hd, tm, stride=KV_HEADS), :] = vh
        vt_ref[0, hd, :V_DIM, :] = vh.T.astype(BF16)
        vt_ref[0, hd, V_DIM:, :] = ones
    gs_ref[...] = proj(gs_ref.shape[-1])
    ga_ref[...] = proj(ga_ref.shape[-1])
    dt_ref[...] = proj(dt_ref.shape[-1])


def _in_proj(x, g, w, tables, qg, kg, bd, widths, tm, seq):
    rows, d = x.shape
    nt = seq // tm
    wz, wxbc, wq, wk, wv, wgs, wga, wdt = widths
    row = lambda n: pl.BlockSpec((tm, n), lambda i: (i, 0))
    tab = pl.BlockSpec((tm, LANES), lambda i: (i % nt, 0))
    f32 = lambda n: jax.ShapeDtypeStruct((rows, n), F32)
    b16 = lambda n: jax.ShapeDtypeStruct((rows, n), BF16)
    vt_rows = V_DIM + ONES_ROWS
    assert wk == KV_HEADS * LANES and wv == KV_HEADS * V_DIM and V_DIM == LANES
    head_rows = pl.BlockSpec((tm * KV_HEADS, LANES), lambda i: (i, 0))
    head_rows_shape = jax.ShapeDtypeStruct((rows * KV_HEADS, LANES), F32)
    return pl.pallas_call(
        _inproj_body,
        grid=(rows // tm,),
        in_specs=[row(d), _resident(g.shape), _resident(w.shape), tab, tab, tab,
                  _resident(qg.shape), _resident(kg.shape), _resident(bd.shape)],
        out_specs=[row(wz), row(wxbc), row(wq), head_rows, row(wk), head_rows,
                   pl.BlockSpec((1, KV_HEADS, vt_rows, tm), lambda i: (i // nt, 0, 0, i % nt)),
                   row(wgs), row(wga), row(wdt)],
        out_shape=[f32(wz), f32(wxbc), b16(wq), head_rows_shape, b16(wk), head_rows_shape,
                   jax.ShapeDtypeStruct((rows // seq, KV_HEADS, vt_rows, seq), BF16),
                   f32(wgs), f32(wga), f32(wdt)],
        compiler_params=_params(("parallel",), 52),
        name="in_proj",
    )(x, g, w, *tables, qg, kg, bd)


def _rope_tables(pos):
    half = ROT_DIM // 2
    inv = ROPE_THETA ** (-jnp.arange(half, dtype=F32) * 2.0 / ROT_DIM)
    ang = pos.astype(F32)[:, None] * inv[None, :]
    cos, sin = jnp.cos(ang), jnp.sin(ang)
    n = pos.shape[0]
    rest = QK_DIM - ROT_DIM
    cos_h = jnp.concatenate([cos, cos, jnp.ones((n, rest), F32)], axis=1)
    sa_h = jnp.concatenate([-sin, jnp.zeros((n, half + rest), F32)], axis=1)
    sb_h = jnp.concatenate([jnp.zeros((n, half), F32), sin, jnp.zeros((n, rest), F32)], axis=1)
    rep = LANES // QK_DIM
    return tuple(jnp.tile(t, (1, rep)) for t in (cos_h, sa_h, sb_h))


def _ssd_body(z_ref, xbc_ref, dt_ref, cw_ref, cb_ref, dtb_ref, a_ref, e_ref, dsk_ref, ng_ref,
              y_ref, ctail_ref, st_ref, xpad_scr, h_scr, *, ch, d_inner):
    c = pl.program_id(1)
    nh_g = d_inner // SSM_GROUPS
    pad = SUBLANES

    @pl.when(c == 0)
    def _():
        xpad_scr[0:pad, :] = jnp.zeros((pad, xpad_scr.shape[1]), F32)
        h_scr[...] = jnp.zeros(h_scr.shape, F32)

    xbc = xbc_ref[...]
    xpad_scr[pad:pad + ch, :] = xbc
    cw = cw_ref[...]
    conv = cb_ref[...] + cw[SSM_CONV - 1:SSM_CONV] * xbc
    for kk in range(SSM_CONV - 1):
        start = pad - (SSM_CONV - 1) + kk
        conv = conv + cw[kk:kk + 1] * xpad_scr[start:start + ch, :]
    xpad_scr[0:pad, :] = xpad_scr[ch:ch + pad, :]
    act = _silu(conv)
    xs = act[:, :d_inner]
    bmb = act[:, d_inner:d_inner + LANES].astype(BF16)
    cmb = act[:, d_inner + LANES:d_inner + 2 * LANES].astype(BF16)

    dt = _softplus(dt_ref[...] + dtb_ref[...])
    da = dt * a_ref[...]
    row = lax.broadcasted_iota(jnp.int32, (ch, ch), 0)
    col = lax.broadcasted_iota(jnp.int32, (ch, ch), 1)
    causal = row >= col
    da3 = jnp.concatenate(_split3(da), axis=1)
    cs3 = jnp.dot(causal.astype(BF16), da3, preferred_element_type=F32)
    cs = cs3[:, :LANES] + cs3[:, LANES:2 * LANES] + cs3[:, 2 * LANES:]
    cs_t = cs.T
    e3 = e_ref[...]
    dt_e = _select_matmul(dt, e3)
    off_scale = _select_matmul(jnp.exp(cs), e3)
    to_end_e = _select_matmul(jnp.exp(cs[ch - 1:ch, :] - cs), e3)
    xdt = xs * dt_e
    xdtb = xdt.astype(BF16)
    wendb = (xdt * to_end_e).astype(BF16)
    cdec = off_scale[ch - 1:ch, :]

    lane = lax.broadcasted_iota(jnp.int32, (ch, LANES), 1)
    lo = lane < SSM_HEAD_DIM
    zero_b = jnp.zeros((ch, LANES), BF16)
    heads_g = nh_g // SSM_HEAD_DIM
    ys = []
    for g in range(SSM_GROUPS):
        bg = bmb[:, g * SSM_STATE:(g + 1) * SSM_STATE]
        cg = cmb[:, g * SSM_STATE:(g + 1) * SSM_STATE]
        cb = lax.dot_general(cg, bg, NT_DIMS, preferred_element_type=F32)
        hg = h_scr[g]
        yoff = jnp.dot(cg, hg.astype(BF16), preferred_element_type=F32)
        s_new = lax.dot_general(bg, wendb[:, g * nh_g:(g + 1) * nh_g], TN_DIMS, preferred_element_type=F32)
        h_scr[g] = hg * cdec[:, g * nh_g:(g + 1) * nh_g] + s_new
        for pr in range(heads_g // 2):
            r0 = g * heads_g + 2 * pr
            gs = []
            for r in (r0, r0 + 1):
                seg = cs[:, r:r + 1] - cs_t[r:r + 1, :]
                gs.append(jnp.where(causal, cb * jnp.exp(seg), 0.0))
            gc = jnp.concatenate(gs, axis=1).astype(BF16)
            blk = xdtb[:, r0 * SSM_HEAD_DIM:(r0 + 2) * SSM_HEAD_DIM]
            rhs = jnp.concatenate([jnp.where(lo, blk, zero_b), jnp.where(lo, zero_b, blk)], axis=0)
            yd = jnp.dot(gc, rhs, preferred_element_type=F32)
            sl = slice(r0 * SSM_HEAD_DIM, (r0 + 2) * SSM_HEAD_DIM)
            ys.append(yd + yoff[:, 2 * pr * SSM_HEAD_DIM:(2 * pr + 2) * SSM_HEAD_DIM] * off_scale[:, sl])
    y = jnp.concatenate(ys, axis=1) + xs * dsk_ref[...]
    yg = y * _silu(z_ref[...])
    outs = [_rms(yg[:, g * nh_g:(g + 1) * nh_g]) for g in range(SSM_GROUPS)]
    y_ref[...] = (jnp.concatenate(outs, axis=1) * ng_ref[...]).astype(BF16)

    @pl.when(c == pl.num_programs(1) - 1)
    def _():
        ctail_ref[0] = xpad_scr[0:pad, :]
        st_ref[0] = h_scr[...]


def _ssd_prompt(z, xbc, dt, cw, cb, dtb, a, e, dsk, ng, batch, seq):
    ch = SSM_CHUNK
    nc = seq // ch
    d_inner = z.shape[1]
    conv_dim = xbc.shape[1]
    nh_g = d_inner // SSM_GROUPS
    row_spec = lambda n: pl.BlockSpec((ch, n), lambda b, c: (b * nc + c, 0))
    return pl.pallas_call(
        functools.partial(_ssd_body, ch=ch, d_inner=d_inner),
        grid=(batch, nc),
        in_specs=[row_spec(d_inner), row_spec(conv_dim), row_spec(LANES)]
        + [_resident(t.shape) for t in (cw, cb, dtb, a, e, dsk, ng)],
        out_specs=[row_spec(d_inner),
                   pl.BlockSpec((1, SUBLANES, conv_dim), lambda b, c: (b, 0, 0)),
                   pl.BlockSpec((1, SSM_GROUPS, SSM_STATE, nh_g), lambda b, c: (b, 0, 0, 0))],
        out_shape=[jax.ShapeDtypeStruct((batch * seq, d_inner), BF16),
                   jax.ShapeDtypeStruct((batch, SUBLANES, conv_dim), F32),
                   jax.ShapeDtypeStruct((batch, SSM_GROUPS, SSM_STATE, nh_g), F32)],
        scratch_shapes=[pltpu.VMEM((SUBLANES + ch, conv_dim), F32),
                        pltpu.VMEM((SSM_GROUPS, SSM_STATE, nh_g), F32)],
        compiler_params=_params(("parallel", "arbitrary"), 40),
        name="ssd_prompt",
    )(z, xbc, dt, cw, cb, dtb, a, e, dsk, ng)


def _lambda(lq1_ref, lk1_ref, lq2_ref, lk2_ref, lam_init):
    s1 = jnp.sum(lq1_ref[...] * lk1_ref[...], axis=-1, keepdims=True)
    s2 = jnp.sum(lq2_ref[...] * lk2_ref[...], axis=-1, keepdims=True)
    return jnp.exp(s1) - jnp.exp(s2) + lam_init


def _attn_body(lq1_ref, lk1_ref, lq2_ref, lk2_ref, sg_ref, q_ref, k_ref, vt_ref, o_ref,
               qp_scr, m_scr, acc_scr, sa_scr, sb_scr, *, tq, nq, lam_init):
    bufs = (sa_scr, sb_scr)

    def start_block(qi):
        q = q_ref[qi * tq:(qi + 1) * tq, :]
        lane = lax.broadcasted_iota(jnp.int32, (tq, LANES), 1)
        lo = lane < QK_DIM
        zero_b = jnp.zeros((tq, LANES), BF16)
        for r in range(KV_REP):
            qr = q[:, r * LANES:(r + 1) * LANES]
            qp_scr[(2 * r) * tq:(2 * r + 1) * tq, :] = jnp.where(lo, qr, zero_b)
            qp_scr[(2 * r + 1) * tq:(2 * r + 2) * tq, :] = jnp.where(lo, zero_b, qr)
        m_scr[...] = jnp.full(m_scr.shape, -jnp.inf, F32)
        acc_scr[...] = jnp.zeros(acc_scr.shape, F32)
        scores(0, bufs[0])

    def scores(j, s_scr):
        s_scr[...] = lax.dot_general(k_ref[j * tq:(j + 1) * tq, :], qp_scr[...], NT_DIMS,
                                     preferred_element_type=F32)

    def consume(j, s_scr, diagonal):
        vtb = vt_ref[:, j * tq:(j + 1) * tq]
        m_prev = m_scr[...]
        ps, ms = [], []
        for c in range(2 * KV_REP * tq // LANES):
            cols = slice(c * LANES, (c + 1) * LANES)
            s = s_scr[:, cols]
            if diagonal:
                key = lax.broadcasted_iota(jnp.int32, (tq, LANES), 0)
                qry = lax.broadcasted_iota(jnp.int32, (tq, LANES), 1) + (c * LANES) % tq
                s = jnp.where(key <= qry, s, -jnp.inf)
            m_c = jnp.maximum(m_prev[:, cols], jnp.max(s, axis=0, keepdims=True))
            ps.append(jnp.exp2(s - m_c).astype(BF16))
            ms.append(m_c)
        m_new = jnp.concatenate(ms, axis=1)
        alpha = jnp.exp2(m_prev - m_new)
        pv = jnp.dot(vtb, jnp.concatenate(ps, axis=1), preferred_element_type=F32)
        acc_scr[...] = alpha * acc_scr[...] + pv
        m_scr[...] = m_new

    lam = _lambda(lq1_ref, lk1_ref, lq2_ref, lk2_ref, lam_init)
    for qi in range(nq):
        start_block(qi)
        for j in range(qi + 1):
            if j < qi:
                scores(j + 1, bufs[(j + 1) % 2])
            consume(j, bufs[j % 2], j == qi)
        for r in range(KV_REP):
            c1 = slice((2 * r) * tq, (2 * r + 1) * tq)
            c2 = slice((2 * r + 1) * tq, (2 * r + 2) * tq)
            o = (acc_scr[:V_DIM, c1] / acc_scr[V_DIM:V_DIM + 1, c1]
                 - lam * (acc_scr[:V_DIM, c2] / acc_scr[V_DIM:V_DIM + 1, c2]))
            y = _rms(o, axis=0) * sg_ref[...] * (1.0 - lam_init)
            o_ref[qi * tq:(qi + 1) * tq, r * LANES:(r + 1) * LANES] = y.T.astype(BF16)


def _attn_prompt(lams, sg_col, qb, kb, vt, batch, seq, lam_init, tq):
    nq = seq // tq
    cols = 2 * KV_REP * tq
    att_w = qb.shape[1]
    return pl.pallas_call(
        functools.partial(_attn_body, tq=tq, nq=nq, lam_init=lam_init),
        grid=(batch, KV_HEADS),
        in_specs=[_resident(t.shape) for t in lams] + [_resident(sg_col.shape)] + [
            pl.BlockSpec((seq, KV_REP * LANES), lambda b, h: (b, h)),
            pl.BlockSpec((seq, LANES), lambda b, h: (b, h)),
            pl.BlockSpec((V_DIM + ONES_ROWS, seq), lambda b, h: (b * KV_HEADS + h, 0))],
        out_specs=pl.BlockSpec((seq, KV_REP * LANES), lambda b, h: (b, h)),
        out_shape=jax.ShapeDtypeStruct((batch * seq, att_w), BF16),
        scratch_shapes=[pltpu.VMEM((cols, LANES), BF16), pltpu.VMEM((1, cols), F32),
                        pltpu.VMEM((V_DIM + ONES_ROWS, cols), F32), pltpu.VMEM((tq, cols), F32),
                        pltpu.VMEM((tq, cols), F32)],
        compiler_params=_params(("parallel", "parallel"), 40),
        name="attn_prompt",
    )(*lams, sg_col, qb, kb, vt)


def _ssd_pre_body(xbc_ref, s0_ref, s1_ref, s2_ref, dt_ref, cw_ref, cb_ref, dtb_ref, a_ref, e_ref,
                  xs_ref, bm_ref, cm_ref, xdt_ref, dec_ref, *, d_inner):
    cw = cw_ref[...]
    conv = (cb_ref[...] + cw[0:1] * s0_ref[...] + cw[1:2] * s1_ref[...] + cw[2:3] * s2_ref[...]
            + cw[3:4] * xbc_ref[...])
    act = _silu(conv)
    xs = act[:, :d_inner]
    dt = _softplus(dt_ref[...] + dtb_ref[...])
    e = e_ref[...]
    dt_e = _select_matmul(dt, e)
    dec_e = _select_matmul(jnp.exp(dt * a_ref[...]), e)
    xs_ref[...] = xs
    bm_ref[...] = act[:, d_inner:d_inner + LANES]
    cm_ref[...] = act[:, d_inner + LANES:d_inner + 2 * LANES]
    xdt_ref[...] = xs * dt_e
    dec_ref[...] = dec_e


def _ssd_sample_pre(xbc, s0, s1, s2, dt, cw, cb, dtb, a, e):
    n, d_inner = xbc.shape[0], e.shape[1]
    wide = jax.ShapeDtypeStruct((n, d_inner), F32)
    narrow = jax.ShapeDtypeStruct((n, LANES), F32)
    return pl.pallas_call(
        functools.partial(_ssd_pre_body, d_inner=d_inner),
        out_shape=[wide, narrow, narrow, wide, wide],
        compiler_params=pltpu.CompilerParams(vmem_limit_bytes=32 * MIB),
        name="ssd_sample_pre",
    )(xbc, s0, s1, s2, dt, cw, cb, dtb, a, e)


def _ssd_step_body(h0_ref, xdt_ref, dec_ref, bm_ref, cm_ref, xs_ref, z_ref, dsk_ref, ng_ref,
                   hn_ref, y_ref, *, nh_g):
    xdt, dec, bm, cm = xdt_ref[0], dec_ref[0], bm_ref[0], cm_ref[0]
    krows = 2 * SUBLANES

    def tile_rows(rows):
        w = rows[0].shape[1]
        return jnp.concatenate([r.astype(BF16) for r in rows] + [jnp.zeros((krows - len(rows), w), BF16)], axis=0)

    ones3 = tile_rows([jnp.ones((1, SSM_STATE), F32)] * 3)
    ys = []
    for g in range(SSM_GROUPS):
        h0 = h0_ref[0, 0, g].reshape(nh_g, SSM_STATE)
        bg = bm[:, g * SSM_STATE:(g + 1) * SSM_STATE]
        cg = cm[:, g * SSM_STATE:(g + 1) * SSM_STATE]
        xg = xdt[:, g * nh_g:(g + 1) * nh_g]
        dg = dec[:, g * nh_g:(g + 1) * nh_g]
        outer = lax.dot_general(tile_rows([xg]), tile_rows([bg]), TN_DIMS, preferred_element_type=F32)
        dcol = lax.dot_general(tile_rows(list(_split3(dg))), ones3, TN_DIMS, preferred_element_type=F32)
        hn_ref[0, g] = (h0 * dcol + outer).reshape(hn_ref.shape[2:])
        ch0 = lax.dot_general(tile_rows([cg]), h0.astype(BF16), NT_DIMS, preferred_element_type=F32)[0:1]
        ys.append(dg * ch0 + xg * jnp.sum(cg * bg, axis=-1, keepdims=True))
    y = jnp.concatenate(ys, axis=1) + xs_ref[0] * dsk_ref[...]
    yg = y * _silu(z_ref[0])
    outs = [_rms(yg[:, g * nh_g:(g + 1) * nh_g]) for g in range(SSM_GROUPS)]
    y_ref[0] = (jnp.concatenate(outs, axis=1) * ng_ref[...]).astype(BF16)


def _ssd_sample_step(state, layer, xdt, dec, bm, cm, xs, z, dsk, ng):
    n = state.shape[1]
    st_dims = state.shape[2:]
    d_inner = xs.shape[-1]
    row3 = lambda w: pl.BlockSpec((1, 1, w), lambda b: (b, 0, 0))
    r3 = lambda t: t.reshape(n, 1, t.shape[-1])
    return pl.pallas_call(
        functools.partial(_ssd_step_body, nh_g=d_inner // SSM_GROUPS),
        grid=(n,),
        in_specs=[pl.BlockSpec((1, 1) + st_dims, lambda b: (layer, b, 0, 0, 0, 0)),
                  row3(d_inner), row3(d_inner), row3(LANES), row3(LANES), row3(d_inner), row3(d_inner),
                  _resident(dsk.shape), _resident(ng.shape)],
        out_specs=[pl.BlockSpec((1,) + st_dims, lambda b: (b, 0, 0, 0, 0)), row3(d_inner)],
        out_shape=[jax.ShapeDtypeStruct((n,) + st_dims, F32), jax.ShapeDtypeStruct((n, 1, d_inner), BF16)],
        compiler_params=_params(("parallel",), 32),
        name="ssd_sample_step",
    )(state, r3(xdt), r3(dec), r3(bm), r3(cm), r3(xs), r3(z), dsk, ng)


def _attn_sample_body(pt_ref, lq1_ref, lk1_ref, lq2_ref, lk2_ref, sg_ref, q_ref, kn_ref, vn_ref, *rest,
                      pages, lam_init):
    k_refs, v_refs = rest[:pages], rest[pages:2 * pages]
    o_ref, qp_scr, bias_scr, m_scr, l_scr, acc_scr = rest[2 * pages:]
    j = pl.program_id(1)
    prow = PAGE_SIZE * KV_HEADS

    def head_rows(row128):
        per_head = [jnp.broadcast_to(row128[:, h * LANES:(h + 1) * LANES], (2, LANES)) for h in range(KV_HEADS)]
        return jnp.concatenate(per_head * KV_REP, axis=0)

    @pl.when(j == 0)
    def _():
        q = q_ref[0].astype(F32)
        lane = lax.broadcasted_iota(jnp.int32, (2, LANES), 1)
        comp = lax.broadcasted_iota(jnp.int32, (2, LANES), 0)
        own_half = (lane // QK_DIM) == comp
        rows = []
        for r in range(KV_REP):
            for h in range(KV_HEADS):
                qhr = q[:, (KV_REP * h + r) * LANES:(KV_REP * h + r + 1) * LANES]
                rows.append(jnp.where(own_half, jnp.broadcast_to(qhr, (2, LANES)), 0.0))
        qp_scr[...] = jnp.concatenate(rows, axis=0).astype(BF16)
        row_head = (lax.broadcasted_iota(jnp.int32, bias_scr.shape, 0) // 2) % KV_HEADS
        col_head = lax.broadcasted_iota(jnp.int32, bias_scr.shape, 1) % KV_HEADS
        bias_scr[...] = jnp.where(row_head == col_head, 0.0, -jnp.inf)
        m_scr[...] = jnp.full(m_scr.shape, -jnp.inf, F32)
        l_scr[...] = jnp.zeros(l_scr.shape, F32)
        acc_scr[...] = jnp.zeros(acc_scr.shape, F32)

    qp = qp_scr[...]
    s = jnp.concatenate(
        [lax.dot_general(qp, k_refs[i][...].astype(BF16), NT_DIMS, preferred_element_type=F32)
         for i in range(pages)], axis=1) + bias_scr[...]
    m_prev = m_scr[...]
    m_new = jnp.maximum(m_prev, jnp.max(s, axis=-1, keepdims=True))
    alpha = jnp.exp2(m_prev - m_new)
    p = jnp.exp2(s - m_new)
    l_scr[...] = alpha * l_scr[...] + jnp.sum(p, axis=-1, keepdims=True)
    pb = p.astype(BF16)
    pv = jnp.dot(pb[:, 0:prow], v_refs[0][...].astype(BF16), preferred_element_type=F32)
    for i in range(1, pages):
        pv = pv + jnp.dot(pb[:, i * prow:(i + 1) * prow], v_refs[i][...].astype(BF16),
                          preferred_element_type=F32)
    acc_scr[...] = alpha * acc_scr[...] + pv
    m_scr[...] = m_new

    @pl.when(j == pl.num_programs(1) - 1)
    def _():
        k_self = head_rows(kn_ref[0]).astype(BF16).astype(F32)
        s_self = jnp.sum(qp.astype(F32) * k_self, axis=-1, keepdims=True)
        m_prev = m_scr[...]
        m_fin = jnp.maximum(m_prev, s_self)
        alpha = jnp.exp2(m_prev - m_fin)
        p_self = jnp.exp2(s_self - m_fin)
        l_fin = alpha * l_scr[...] + p_self
        acc = (alpha * acc_scr[...] + p_self * head_rows(vn_ref[0])) / l_fin
        lam = _lambda(lq1_ref, lk1_ref, lq2_ref, lk2_ref, lam_init)
        for h in range(KV_HEADS):
            for r in range(KV_REP):
                i1 = (r * KV_HEADS + h) * 2
                o = acc[i1:i1 + 1, :] - lam * acc[i1 + 1:i1 + 2, :]
                y = _rms(o) * sg_ref[...] * (1.0 - lam_init)
                dst = (KV_REP * h + r) * LANES
                o_ref[0, :, dst:dst + LANES] = y.astype(BF16)


def _attn_sample(page_table, lams, sg, qb, k_new, v_new, cache_k, cache_v, layer, lam_init):
    n, n_pages = page_table.shape
    pages = PAGES_PER_STEP
    while n_pages % pages:
        pages //= 2
    att_w = qb.shape[-1]
    kw = KV_HEADS * LANES
    nrow = 2 * KV_REP * KV_HEADS
    n_pool = cache_k.shape[1]
    prow = PAGE_SIZE * KV_HEADS
    ck = cache_k.reshape(cache_k.shape[0] * n_pool * prow, 2 * QK_DIM)
    cv = cache_v.reshape(cache_v.shape[0] * n_pool * prow, V_DIM)
    page_table = page_table + layer * n_pool
    const = lambda shape: pl.BlockSpec(shape, lambda b, j, pt: (0,) * len(shape))
    row3 = lambda w: pl.BlockSpec((1, 1, w), lambda b, j, pt: (b, 0, 0))

    def page_spec(i):
        return pl.BlockSpec((prow, LANES), lambda b, j, pt: (pt[b * n_pages + j * pages + i], 0))

    grid_spec = pltpu.PrefetchScalarGridSpec(
        num_scalar_prefetch=1,
        grid=(n, n_pages // pages),
        in_specs=[const(t.shape) for t in lams] + [const(sg.shape), row3(att_w), row3(kw), row3(kw)]
        + [page_spec(i) for i in range(pages)] + [page_spec(i) for i in range(pages)],
        out_specs=row3(att_w),
        scratch_shapes=[pltpu.VMEM((nrow, LANES), BF16), pltpu.VMEM((nrow, pages * prow), F32),
                        pltpu.VMEM((nrow, 1), F32), pltpu.VMEM((nrow, 1), F32), pltpu.VMEM((nrow, LANES), F32)],
    )
    return pl.pallas_call(
        functools.partial(_attn_sample_body, pages=pages, lam_init=lam_init),
        grid_spec=grid_spec,
        out_shape=jax.ShapeDtypeStruct((n, 1, att_w), BF16),
        compiler_params=_params(("parallel", "arbitrary"), 40),
        name="attn_sample",
    )(page_table.reshape(-1), *lams, sg, qb.reshape(n, 1, att_w), k_new.reshape(n, 1, kw),
      v_new.reshape(n, 1, kw), *([ck] * pages), *([cv] * pages))


def _merge_residual_norm(ys_ref, ya_ref, gs_ref, ga_ref, x_ref, wso_ref, wao_ref, wo_ref, g_ref):
    o_ssd = jnp.dot(ys_ref[...], wso_ref[...], preferred_element_type=F32)
    o_att = jnp.dot(ya_ref[...], wao_ref[...], preferred_element_type=F32)
    merged = _sigmoid(gs_ref[...]) * o_ssd + _sigmoid(ga_ref[...]) * o_att
    x1 = x_ref[...] + jnp.dot(merged.astype(BF16), wo_ref[...], preferred_element_type=F32)
    return x1, (_rms(x1) * g_ref[...]).astype(BF16)


def _outproj_body(ys_ref, ya_ref, gs_ref, ga_ref, x_ref, wso_ref, wao_ref, wo_ref, g_ref, x1_ref, hn_ref):
    x1, hn = _merge_residual_norm(ys_ref, ya_ref, gs_ref, ga_ref, x_ref, wso_ref, wao_ref, wo_ref, g_ref)
    x1_ref[...] = x1
    hn_ref[...] = hn


def _out_proj(ys, ya, gs, ga, x, wso, wao, wo, g, tm):
    rows, d = x.shape
    row_spec = lambda n: pl.BlockSpec((tm, n), lambda i: (i, 0))
    return pl.pallas_call(
        _outproj_body,
        grid=(rows // tm,),
        in_specs=[row_spec(ys.shape[1]), row_spec(ya.shape[1]), row_spec(d), row_spec(d), row_spec(d)]
        + [_resident(t.shape) for t in (wso, wao, wo, g)],
        out_specs=[row_spec(d), row_spec(d)],
        out_shape=[jax.ShapeDtypeStruct((rows, d), F32), jax.ShapeDtypeStruct((rows, d), BF16)],
        compiler_params=_params(("parallel",), 48),
        name="out_proj",
    )(ys, ya, gs, ga, x, wso, wao, wo, g)


def _ffn_chunks(dff):
    tf = 2 * LANES
    assert dff % tf == 0
    return tf, dff // tf


def _mixer_ffn_body(ys_ref, ya_ref, gs_ref, ga_ref, x_ref, wso_ref, wao_ref, wo_ref, g_ref,
                    wup_ref, wdn_ref, cw_ref, cb_ref, y_ref, tail_ref, ubuf_scr, act_scr, *, tm, dff):
    t = pl.program_id(1)
    pad = SUBLANES

    @pl.when(t == 0)
    def _():
        ubuf_scr[0:pad, :] = jnp.zeros((pad, ubuf_scr.shape[1]), F32)

    x1, hn = _merge_residual_norm(ys_ref, ya_ref, gs_ref, ga_ref, x_ref, wso_ref, wao_ref, wo_ref, g_ref)
    tf, nchunk = _ffn_chunks(dff)
    for c in range(nchunk):
        halves = []
        for half in range(2):
            sl = slice(half * dff + c * tf, half * dff + (c + 1) * tf)
            u = jnp.dot(hn, wup_ref[:, sl], preferred_element_type=F32)
            ubuf_scr[pad:pad + tm, sl] = u
            cw = cw_ref[:, sl]
            conv = cb_ref[:, sl] + cw[FFN_CONV - 1:FFN_CONV] * u
            for kk in range(FFN_CONV - 1):
                start = pad - (FFN_CONV - 1) + kk
                conv = conv + cw[kk:kk + 1] * ubuf_scr[start:start + tm, sl]
            ubuf_scr[0:pad, sl] = ubuf_scr[tm:tm + pad, sl]
            halves.append(conv)
        act_scr[:, c * tf:(c + 1) * tf] = (_silu(halves[1]) * halves[0]).astype(BF16)
    y_ref[...] = x1 + jnp.dot(act_scr[...], wdn_ref[...], preferred_element_type=F32)

    @pl.when(t == pl.num_programs(1) - 1)
    def _():
        tail_ref[0] = ubuf_scr[0:pad, :]


def _mixer_ffn_prompt(ys, ya, gs, ga, x, wso, wao, wo, g, wup, wdn, cw, cb, batch, seq, tm):
    d = x.shape[1]
    dff = wdn.shape[0]
    nt = seq // tm
    row_spec = lambda n: pl.BlockSpec((tm, n), lambda b, t: (b * nt + t, 0))
    return pl.pallas_call(
        functools.partial(_mixer_ffn_body, tm=tm, dff=dff),
        grid=(batch, nt),
        in_specs=[row_spec(ys.shape[1]), row_spec(ya.shape[1]), row_spec(d), row_spec(d), row_spec(d)]
        + [_resident(t.shape) for t in (wso, wao, wo, g, wup, wdn, cw, cb)],
        out_specs=[row_spec(d), pl.BlockSpec((1, SUBLANES, 2 * dff), lambda b, t: (b, 0, 0))],
        out_shape=[jax.ShapeDtypeStruct(x.shape, F32), jax.ShapeDtypeStruct((batch, SUBLANES, 2 * dff), F32)],
        scratch_shapes=[pltpu.VMEM((SUBLANES + tm, 2 * dff), F32), pltpu.VMEM((tm, dff), BF16)],
        compiler_params=_params(("parallel", "arbitrary"), 56),
        name="mixer_ffn_prompt",
    )(ys, ya, gs, ga, x, wso, wao, wo, g, wup, wdn, cw, cb)


def _ffn_sample_body(hn_ref, x1_ref, s0_ref, s1_ref, wup_ref, wdn_ref, cw_ref, cb_ref, y_ref, u_ref, *, dff):
    hn = hn_ref[...]
    tf, nchunk = _ffn_chunks(dff)
    acc = jnp.zeros(y_ref.shape, F32)
    for c in range(nchunk):
        halves = []
        for half in range(2):
            sl = slice(half * dff + c * tf, half * dff + (c + 1) * tf)
            u = jnp.dot(hn, wup_ref[:, sl], preferred_element_type=F32)
            u_ref[:, sl] = u
            cw = cw_ref[:, sl]
            halves.append(cb_ref[:, sl] + cw[0:1] * s0_ref[:, sl] + cw[1:2] * s1_ref[:, sl] + cw[2:3] * u)
        act = (_silu(halves[1]) * halves[0]).astype(BF16)
        acc = acc + jnp.dot(act, wdn_ref[c * tf:(c + 1) * tf, :], preferred_element_type=F32)
    y_ref[...] = x1_ref[...] + acc


def _ffn_sample(hn, x1, s0, s1, wup, wdn, cw, cb):
    dff = wdn.shape[0]
    return pl.pallas_call(
        functools.partial(_ffn_sample_body, dff=dff),
        out_shape=[jax.ShapeDtypeStruct(x1.shape, F32), jax.ShapeDtypeStruct((x1.shape[0], 2 * dff), F32)],
        compiler_params=pltpu.CompilerParams(vmem_limit_bytes=52 * MIB),
        name="ffn_sample",
    )(hn, x1, s0, s1, wup, wdn, cw, cb)


def _largest_tile(n, cap):
    t = cap
    while n % t:
        t //= 2
    return t


def kernel(x_prompt, x_sample, cache_k, cache_v, state_conv, state_ssm, state_ffn_conv, page_table, norm_mix_g, w_in, ssm_conv_w, ssm_conv_b, ssm_dt_bias, ssm_a_log, ssm_d, ssm_norm_g, q_norm_g, k_norm_g, lambda_q1, lambda_k1, lambda_q2, lambda_k2, attn_subln_g, w_ssd_o, w_attn_o, w_o, norm_ffn_g, w_ffn_up, ffn_conv_w, ffn_conv_b, w_ffn_down):
    bp, sp, d = x_prompt.shape
    bs, ss, _ = x_sample.shape
    assert ss == 1, "the sample group decodes one token per sequence"
    depth = w_in.shape[0]
    past = page_table.shape[1] * PAGE_SIZE
    d_inner = w_ssd_o.shape[1]
    n_heads = ssm_dt_bias.shape[1]
    conv_dim = ssm_conv_w.shape[2]
    att_w = w_attn_o.shape[1]
    kw = KV_HEADS * 2 * QK_DIM
    vw = KV_HEADS * V_DIM
    dff = w_ffn_down.shape[1]
    heads_g = n_heads // SSM_GROUPS
    nh_g = d_inner // SSM_GROUPS

    sizes = (d_inner, conv_dim, n_heads, att_w, kw, vw, d, d)
    cuts = np.concatenate([[0], np.cumsum(sizes)])
    order = (0, 1, 3, 4, 5, 6, 7, 2)
    widths = tuple(sizes[i] for i in order[:-1]) + (LANES,)

    tables_p = _rope_tables(jnp.arange(sp))
    tables_s = _rope_tables(jnp.broadcast_to(past + jnp.arange(ss), (bs,)))
    bd = jnp.asarray(np.kron(np.eye(LANES // QK_DIM), np.ones((QK_DIM, QK_DIM))), BF16)
    expand = jnp.asarray(np.tile(np.pad(np.kron(np.eye(n_heads), np.ones((1, SSM_HEAD_DIM))),
                                        ((0, LANES - n_heads), (0, 0))), (3, 1)), BF16)
    pad_heads = lambda v: jnp.pad(v.astype(F32), (0, LANES - n_heads)).reshape(1, LANES)

    xp = x_prompt.reshape(bp * sp, d)
    xs_ = x_sample.reshape(bs * ss, d)
    tm_p = _largest_tile(bp * sp, 256)
    tq = _largest_tile(sp, 256)
    tm_f = _largest_tile(sp, 256)

    outs = [[] for _ in range(10)]
    for layer in range(depth):
        lam_init = 0.8 - 0.6 * math.exp(-0.3 * layer)
        wl = w_in[layer]
        w_perm = jnp.concatenate([wl[:, cuts[i]:cuts[i + 1]] for i in order]
                                 + [jnp.zeros((d, LANES - n_heads), F32)], axis=1).astype(BF16)
        g_mix = norm_mix_g[layer].reshape(1, d)
        cw, cb = ssm_conv_w[layer], ssm_conv_b[layer].reshape(1, conv_dim)
        dtb = pad_heads(ssm_dt_bias[layer])
        a_neg = pad_heads(-jnp.exp(ssm_a_log[layer].astype(F32)))
        dsk = jnp.repeat(ssm_d[layer].astype(F32), SSM_HEAD_DIM).reshape(1, d_inner)
        ng = ssm_norm_g[layer].reshape(1, d_inner)
        qg = jnp.tile(q_norm_g[layer], LANES // QK_DIM).reshape(1, LANES)
        kg = jnp.tile(k_norm_g[layer], LANES // QK_DIM).reshape(1, LANES)
        lams = tuple(t[layer].reshape(1, QK_DIM) for t in (lambda_q1, lambda_k1, lambda_q2, lambda_k2))
        sg = attn_subln_g[layer].reshape(1, V_DIM)
        wso, wao, wo = (t[layer].astype(BF16) for t in (w_ssd_o, w_attn_o, w_o))
        g_ffn = norm_ffn_g[layer].reshape(1, d)
        wup, wdn = w_ffn_up[layer].astype(BF16), w_ffn_down[layer].astype(BF16)
        fcw, fcb = ffn_conv_w[layer], ffn_conv_b[layer].reshape(1, 2 * dff)

        z, xbc, qb, kf, kb, v, vt, gs, ga, dt = _in_proj(xp, g_mix, w_perm, tables_p, qg, kg, bd, widths, tm_p, sp)
        y_ssd, ctail, st = _ssd_prompt(z, xbc, dt, cw, cb, dtb, a_neg, expand, dsk, ng, bp, sp)
        vt = vt.reshape(bp * KV_HEADS * (V_DIM + ONES_ROWS), sp)
        y_att = _attn_prompt(lams, sg.reshape(V_DIM, 1), qb, kb, vt, bp, sp, lam_init, tq)
        xp, ftail = _mixer_ffn_prompt(y_ssd, y_att, gs, ga, xp, wso, wao, wo, g_ffn, wup, wdn, fcw, fcb,
                                      bp, sp, tm_f)
        outs[0].append(kf.reshape(bp, sp, KV_HEADS, 2 * QK_DIM))
        outs[1].append(v.reshape(bp, sp, KV_HEADS, V_DIM))
        outs[2].append(ctail[:, SUBLANES - (SSM_CONV - 1):, :])
        outs[3].append(st.reshape(bp, SSM_GROUPS, SSM_STATE, heads_g, SSM_HEAD_DIM).transpose(0, 1, 3, 4, 2))
        outs[4].append(ftail[:, SUBLANES - (FFN_CONV - 1):, :])

        z, xbc, qb, kf, _, v, _, gs, ga, dt = _in_proj(xs_, g_mix, w_perm, tables_s, qg, kg, bd, widths, bs, bs)
        sc = state_conv[layer]
        xs_c, bm, cm, xdt, dec = _ssd_sample_pre(xbc, sc[:, 0], sc[:, 1], sc[:, 2], dt, cw, cb, dtb, a_neg, expand)
        h_new, y_ssd = _ssd_sample_step(state_ssm, layer, xdt, dec, bm, cm, xs_c, z, dsk, ng)
        y_att = _attn_sample(page_table, lams, sg, qb, kf, v, cache_k, cache_v, layer, lam_init)
        x1, hn = _out_proj(y_ssd.reshape(bs, d_inner), y_att.reshape(bs, att_w), gs, ga, xs_, wso, wao, wo, g_ffn, bs)
        sf = state_ffn_conv[layer]
        xs_, u_new = _ffn_sample(hn, x1, sf[:, 0], sf[:, 1], wup, wdn, fcw, fcb)
        outs[5].append(kf.reshape(bs, ss, KV_HEADS, 2 * QK_DIM))
        outs[6].append(v.reshape(bs, ss, KV_HEADS, V_DIM))
        outs[7].append(jnp.concatenate([sc[:, 1:], xbc[:, None, :]], axis=1))
        outs[8].append(h_new)
        outs[9].append(jnp.concatenate([sf[:, 1:], u_new[:, None, :]], axis=1))

    stacked = [jnp.stack(o) for o in outs]
    return (xp.reshape(bp, sp, d), xs_.reshape(bs, ss, d), *stacked)
```

```python
import functools
import math

import jax
import jax.numpy as jnp
import numpy as np
from jax import lax
from jax.experimental import pallas as pl
from jax.experimental.pallas import tpu as pltpu

F32 = jnp.float32
BF16 = jnp.bfloat16

EPS = 1e-6
PAGE_SIZE = 128
SSM_HEAD_DIM = 64
SSM_GROUPS = 2
SSM_STATE = 64
SSM_CONV = 4
SSM_CHUNK = 128
KV_HEADS = 4
KV_REP = 2
QK_DIM = 64
V_DIM = 128
ROT_DIM = QK_DIM // 4
ROPE_THETA = 500000.0
FFN_CONV = 3
QK_SCALE = QK_DIM ** -0.5 * math.log2(math.e)
ONES_ROWS = 16
LANES = 128
SUBLANES = 8
SCORE_BUFFERS = 2
PAGES_PER_STEP = 32
MIB = 1024 * 1024

NT_DIMS = (((1,), (1,)), ((), ()))
TN_DIMS = (((0,), (0,)), ((), ()))


def _params(semantics, vmem_mib):
    return pltpu.CompilerParams(dimension_semantics=semantics, vmem_limit_bytes=vmem_mib * MIB)


def _resident(shape):
    nd = len(shape)
    return pl.BlockSpec(shape, lambda *_: (0,) * nd, pipeline_mode=pl.Buffered(1))


def _sigmoid(x):
    return 1.0 / (1.0 + jnp.exp(-x))


def _silu(x):
    h = 0.5 * x
    return h + h * jnp.tanh(h)


def _softplus(x):
    return jnp.maximum(x, 0.0) + jnp.log1p(jnp.exp(-jnp.abs(x)))


def _split3(x):
    hi = x.astype(BF16)
    r1 = x - hi.astype(F32)
    mid = r1.astype(BF16)
    lo = (r1 - mid.astype(F32)).astype(BF16)
    return hi, mid, lo


def _select_matmul(x, sel3):
    return jnp.dot(jnp.concatenate(_split3(x), axis=1), sel3, preferred_element_type=F32)


def _rms(x, axis=-1):
    return x * lax.rsqrt(jnp.mean(x * x, axis=axis, keepdims=True) + EPS)


def _inproj_body(x_ref, g_ref, w_ref, cos_ref, sa_ref, sb_ref, qg_ref, kg_ref, bd_ref,
                 z_ref, xbc_ref, qb_ref, kf_ref, kb_ref, v_ref, vt_ref, gs_ref, ga_ref, dt_ref):
    h = (_rms(x_ref[...]) * g_ref[...]).astype(BF16)
    cos, sa, sb, bd = cos_ref[...], sa_ref[...], sb_ref[...], bd_ref[...]
    off = [0]

    def proj(n):
        u = jnp.dot(h, w_ref[:, off[0]:off[0] + n], preferred_element_type=F32)
        off[0] += n
        return u

    def norm_rope(xb, g):
        ms = jnp.dot((xb * xb).astype(BF16), bd, preferred_element_type=F32) * (1.0 / QK_DIM)
        y = xb * lax.rsqrt(ms + EPS) * g
        return y * cos + pltpu.roll(y, LANES - ROT_DIM // 2, 1) * sa + pltpu.roll(y, ROT_DIM // 2, 1) * sb

    z_ref[...] = proj(z_ref.shape[-1])
    xbc_ref[...] = proj(xbc_ref.shape[-1])
    q = proj(qb_ref.shape[-1])
    for j in range(q.shape[-1] // LANES):
        sl = slice(j * LANES, (j + 1) * LANES)
        qb_ref[:, sl] = (norm_rope(q[:, sl], qg_ref[...]) * QK_SCALE).astype(BF16)
    tm = x_ref.shape[0]
    k = proj(kb_ref.shape[-1])
    for hd in range(KV_HEADS):
        sl = slice(hd * LANES, (hd + 1) * LANES)
        kr = norm_rope(k[:, sl], kg_ref[...])
        kf_ref[pl.ds(hd, tm, stride=KV_HEADS), :] = kr
        kb_ref[:, sl] = kr.astype(BF16)
    v = proj(KV_HEADS * V_DIM)
    ones = jnp.ones((ONES_ROWS, tm), BF16)
    for hd in range(KV_HEADS):
        vh = v[:, hd * V_DIM:(hd + 1) * V_DIM]
        v_ref[pl.ds(hd, tm, stride=KV_HEADS), :] = vh
        vt_ref[0, hd, :V_DIM, :] = vh.T.astype(BF16)
        vt_ref[0, hd, V_DIM:, :] = ones
    gs_ref[...] = proj(gs_ref.shape[-1])
    ga_ref[...] = proj(ga_ref.shape[-1])
    dt_ref[...] = proj(dt_ref.shape[-1])


def _in_proj(x, g, w, tables, qg, kg, bd, widths, tm, seq):
    rows, d = x.shape
    nt = seq // tm
    wz, wxbc, wq, wk, wv, wgs, wga, wdt = widths
    row = lambda n: pl.BlockSpec((tm, n), lambda i: (i, 0))
    tab = pl.BlockSpec((tm, LANES), lambda i: (i % nt, 0))
    f32 = lambda n: jax.ShapeDtypeStruct((rows, n), F32)
    b16 = lambda n: jax.ShapeDtypeStruct((rows, n), BF16)
    vt_rows = V_DIM + ONES_ROWS
    assert wk == KV_HEADS * LANES and wv == KV_HEADS * V_DIM and V_DIM == LANES
    head_rows = pl.BlockSpec((tm * KV_HEADS, LANES), lambda i: (i, 0))
    head_rows_shape = jax.ShapeDtypeStruct((rows * KV_HEADS, LANES), F32)
    return pl.pallas_call(
        _inproj_body,
        grid=(rows // tm,),
        in_specs=[row(d), _resident(g.shape), _resident(w.shape), tab, tab, tab,
                  _resident(qg.shape), _resident(kg.shape), _resident(bd.shape)],
        out_specs=[row(wz), row(wxbc), row(wq), head_rows, row(wk), head_rows,
                   pl.BlockSpec((1, KV_HEADS, vt_rows, tm), lambda i: (i // nt, 0, 0, i % nt)),
                   row(wgs), row(wga), row(wdt)],
        out_shape=[f32(wz), f32(wxbc), b16(wq), head_rows_shape, b16(wk), head_rows_shape,
                   jax.ShapeDtypeStruct((rows // seq, KV_HEADS, vt_rows, seq), BF16),
                   f32(wgs), f32(wga), f32(wdt)],
        compiler_params=_params(("parallel",), 52),
        name="in_proj",
    )(x, g, w, *tables, qg, kg, bd)


def _rope_tables(pos):
    half = ROT_DIM // 2
    inv = ROPE_THETA ** (-jnp.arange(half, dtype=F32) * 2.0 / ROT_DIM)
    ang = pos.astype(F32)[:, None] * inv[None, :]
    cos, sin = jnp.cos(ang), jnp.sin(ang)
    n = pos.shape[0]
    rest = QK_DIM - ROT_DIM
    cos_h = jnp.concatenate([cos, cos, jnp.ones((n, rest), F32)], axis=1)
    sa_h = jnp.concatenate([-sin, jnp.zeros((n, half + rest), F32)], axis=1)
    sb_h = jnp.concatenate([jnp.zeros((n, half), F32), sin, jnp.zeros((n, rest), F32)], axis=1)
    rep = LANES // QK_DIM
    return tuple(jnp.tile(t, (1, rep)) for t in (cos_h, sa_h, sb_h))


def _ssd_body(z_ref, xbc_ref, dt_ref, cw_ref, cb_ref, dtb_ref, a_ref, e_ref, dsk_ref, ng_ref,
              y_ref, ctail_ref, st_ref, xpad_scr, h_scr, *, ch, d_inner):
    c = pl.program_id(1)
    nh_g = d_inner // SSM_GROUPS
    pad = SUBLANES

    @pl.when(c == 0)
    def _():
        xpad_scr[0:pad, :] = jnp.zeros((pad, xpad_scr.shape[1]), F32)
        h_scr[...] = jnp.zeros(h_scr.shape, F32)

    xbc = xbc_ref[...]
    xpad_scr[pad:pad + ch, :] = xbc
    cw = cw_ref[...]
    conv = cb_ref[...] + cw[SSM_CONV - 1:SSM_CONV] * xbc
    for kk in range(SSM_CONV - 1):
        start = pad - (SSM_CONV - 1) + kk
        conv = conv + cw[kk:kk + 1] * xpad_scr[start:start + ch, :]
    xpad_scr[0:pad, :] = xpad_scr[ch:ch + pad, :]
    act = _silu(conv)
    xs = act[:, :d_inner]
    bmb = act[:, d_inner:d_inner + LANES].astype(BF16)
    cmb = act[:, d_inner + LANES:d_inner + 2 * LANES].astype(BF16)

    dt = _softplus(dt_ref[...] + dtb_ref[...])
    da = dt * a_ref[...]
    row = lax.broadcasted_iota(jnp.int32, (ch, ch), 0)
    col = lax.broadcasted_iota(jnp.int32, (ch, ch), 1)
    causal = row >= col
    da3 = jnp.concatenate(_split3(da), axis=1)
    cs3 = jnp.dot(causal.astype(BF16), da3, preferred_element_type=F32)
    cs = cs3[:, :LANES] + cs3[:, LANES:2 * LANES] + cs3[:, 2 * LANES:]
    cs_t = cs.T
    e3 = e_ref[...]
    dt_e = _select_matmul(dt, e3)
    off_scale = _select_matmul(jnp.exp(cs), e3)
    to_end_e = _select_matmul(jnp.exp(cs[ch - 1:ch, :] - cs), e3)
    xdt = xs * dt_e
    xdtb = xdt.astype(BF16)
    wendb = (xdt * to_end_e).astype(BF16)
    cdec = off_scale[ch - 1:ch, :]

    lane = lax.broadcasted_iota(jnp.int32, (ch, LANES), 1)
    lo = lane < SSM_HEAD_DIM
    zero_b = jnp.zeros((ch, LANES), BF16)
    heads_g = nh_g // SSM_HEAD_DIM
    ys = []
    for g in range(SSM_GROUPS):
        bg = bmb[:, g * SSM_STATE:(g + 1) * SSM_STATE]
        cg = cmb[:, g * SSM_STATE:(g + 1) * SSM_STATE]
        cb = lax.dot_general(cg, bg, NT_DIMS, preferred_element_type=F32)
        hg = h_scr[g]
        yoff = jnp.dot(cg, hg.astype(BF16), preferred_element_type=F32)
        s_new = lax.dot_general(bg, wendb[:, g * nh_g:(g + 1) * nh_g], TN_DIMS, preferred_element_type=F32)
        h_scr[g] = hg * cdec[:, g * nh_g:(g + 1) * nh_g] + s_new
        for pr in range(heads_g // 2):
            r0 = g * heads_g + 2 * pr
            gs = []
            for r in (r0, r0 + 1):
                seg = cs[:, r:r + 1] - cs_t[r:r + 1, :]
                gs.append(jnp.where(causal, cb * jnp.exp(seg), 0.0))
            gc = jnp.concatenate(gs, axis=1).astype(BF16)
            blk = xdtb[:, r0 * SSM_HEAD_DIM:(r0 + 2) * SSM_HEAD_DIM]
            rhs = jnp.concatenate([jnp.where(lo, blk, zero_b), jnp.where(lo, zero_b, blk)], axis=0)
            yd = jnp.dot(gc, rhs, preferred_element_type=F32)
            sl = slice(r0 * SSM_HEAD_DIM, (r0 + 2) * SSM_HEAD_DIM)
            ys.append(yd + yoff[:, 2 * pr * SSM_HEAD_DIM:(2 * pr + 2) * SSM_HEAD_DIM] * off_scale[:, sl])
    y = jnp.concatenate(ys, axis=1) + xs * dsk_ref[...]
    yg = y * _silu(z_ref[...])
    outs = [_rms(yg[:, g * nh_g:(g + 1) * nh_g]) for g in range(SSM_GROUPS)]
    y_ref[...] = (jnp.concatenate(outs, axis=1) * ng_ref[...]).astype(BF16)

    @pl.when(c == pl.num_programs(1) - 1)
    def _():
        ctail_ref[0] = xpad_scr[0:pad, :]
        st_ref[0] = h_scr[...]


def _ssd_prompt(z, xbc, dt, cw, cb, dtb, a, e, dsk, ng, batch, seq):
    ch = SSM_CHUNK
    nc = seq // ch
    d_inner = z.shape[1]
    conv_dim = xbc.shape[1]
    nh_g = d_inner // SSM_GROUPS
    row_spec = lambda n: pl.BlockSpec((ch, n), lambda b, c: (b * nc + c, 0))
    return pl.pallas_call(
        functools.partial(_ssd_body, ch=ch, d_inner=d_inner),
        grid=(batch, nc),
        in_specs=[row_spec(d_inner), row_spec(conv_dim), row_spec(LANES)]
        + [_resident(t.shape) for t in (cw, cb, dtb, a, e, dsk, ng)],
        out_specs=[row_spec(d_inner),
                   pl.BlockSpec((1, SUBLANES, conv_dim), lambda b, c: (b, 0, 0)),
                   pl.BlockSpec((1, SSM_GROUPS, SSM_STATE, nh_g), lambda b, c: (b, 0, 0, 0))],
        out_shape=[jax.ShapeDtypeStruct((batch * seq, d_inner), BF16),
                   jax.ShapeDtypeStruct((batch, SUBLANES, conv_dim), F32),
                   jax.ShapeDtypeStruct((batch, SSM_GROUPS, SSM_STATE, nh_g), F32)],
        scratch_shapes=[pltpu.VMEM((SUBLANES + ch, conv_dim), F32),
                        pltpu.VMEM((SSM_GROUPS, SSM_STATE, nh_g), F32)],
        compiler_params=_params(("parallel", "arbitrary"), 40),
        name="ssd_prompt",
    )(z, xbc, dt, cw, cb, dtb, a, e, dsk, ng)


def _lambda(lq1_ref, lk1_ref, lq2_ref, lk2_ref, lam_init):
    s1 = jnp.sum(lq1_ref[...] * lk1_ref[...], axis=-1, keepdims=True)
    s2 = jnp.sum(lq2_ref[...] * lk2_ref[...], axis=-1, keepdims=True)
    return jnp.exp(s1) - jnp.exp(s2) + lam_init


def _attn_body(lq1_ref, lk1_ref, lq2_ref, lk2_ref, sg_ref, q_ref, k_ref, vt_ref, o_ref,
               qp_scr, m_scr, acc_scr, *bufs, tq, nq, lam_init):

    def start_block(qi, s_first):
        q = q_ref[qi * tq:(qi + 1) * tq, :]
        lane = lax.broadcasted_iota(jnp.int32, (tq, LANES), 1)
        lo = lane < QK_DIM
        zero_b = jnp.zeros((tq, LANES), BF16)
        for r in range(KV_REP):
            qr = q[:, r * LANES:(r + 1) * LANES]
            qp_scr[(2 * r) * tq:(2 * r + 1) * tq, :] = jnp.where(lo, qr, zero_b)
            qp_scr[(2 * r + 1) * tq:(2 * r + 2) * tq, :] = jnp.where(lo, zero_b, qr)
        m_scr[...] = jnp.full(m_scr.shape, -jnp.inf, F32)
        acc_scr[...] = jnp.zeros(acc_scr.shape, F32)
        scores(0, s_first)

    def scores(j, s_scr):
        s_scr[...] = lax.dot_general(k_ref[j * tq:(j + 1) * tq, :], qp_scr[...], NT_DIMS,
                                     preferred_element_type=F32)

    def consume(j, s_scr, diagonal):
        vtb = vt_ref[:, j * tq:(j + 1) * tq]
        m_prev = m_scr[...]
        ps, ms = [], []
        for c in range(2 * KV_REP * tq // LANES):
            cols = slice(c * LANES, (c + 1) * LANES)
            s = s_scr[:, cols]
            if diagonal:
                key = lax.broadcasted_iota(jnp.int32, (tq, LANES), 0)
                qry = lax.broadcasted_iota(jnp.int32, (tq, LANES), 1) + (c * LANES) % tq
                s = jnp.where(key <= qry, s, -jnp.inf)
            m_c = jnp.maximum(m_prev[:, cols], jnp.max(s, axis=0, keepdims=True))
            ps.append(jnp.exp2(s - m_c).astype(BF16))
            ms.append(m_c)
        m_new = jnp.concatenate(ms, axis=1)
        alpha = jnp.exp2(m_prev - m_new)
        pv = jnp.dot(vtb, jnp.concatenate(ps, axis=1), preferred_element_type=F32)
        acc_scr[...] = alpha * acc_scr[...] + pv
        m_scr[...] = m_new

    lam = _lambda(lq1_ref, lk1_ref, lq2_ref, lk2_ref, lam_init)
    issued = 0
    for qi in range(nq):
        cur = bufs[issued % len(bufs)]
        start_block(qi, cur)
        issued += 1
        for j in range(qi + 1):
            nxt = bufs[issued % len(bufs)]
            if j < qi:
                scores(j + 1, nxt)
                issued += 1
            consume(j, cur, j == qi)
            cur = nxt
        for r in range(KV_REP):
            c1 = slice((2 * r) * tq, (2 * r + 1) * tq)
            c2 = slice((2 * r + 1) * tq, (2 * r + 2) * tq)
            o = (acc_scr[:V_DIM, c1] / acc_scr[V_DIM:V_DIM + 1, c1]
                 - lam * (acc_scr[:V_DIM, c2] / acc_scr[V_DIM:V_DIM + 1, c2]))
            y = _rms(o, axis=0) * sg_ref[...] * (1.0 - lam_init)
            o_ref[qi * tq:(qi + 1) * tq, r * LANES:(r + 1) * LANES] = y.T.astype(BF16)


def _attn_prompt(lams, sg_col, qb, kb, vt, batch, seq, lam_init, tq):
    nq = seq // tq
    cols = 2 * KV_REP * tq
    att_w = qb.shape[1]
    return pl.pallas_call(
        functools.partial(_attn_body, tq=tq, nq=nq, lam_init=lam_init),
        grid=(batch, KV_HEADS),
        in_specs=[_resident(t.shape) for t in lams] + [_resident(sg_col.shape)] + [
            pl.BlockSpec((seq, KV_REP * LANES), lambda b, h: (b, h)),
            pl.BlockSpec((seq, LANES), lambda b, h: (b, h)),
            pl.BlockSpec((V_DIM + ONES_ROWS, seq), lambda b, h: (b * KV_HEADS + h, 0))],
        out_specs=pl.BlockSpec((seq, KV_REP * LANES), lambda b, h: (b, h)),
        out_shape=jax.ShapeDtypeStruct((batch * seq, att_w), BF16),
        scratch_shapes=[pltpu.VMEM((cols, LANES), BF16), pltpu.VMEM((1, cols), F32),
                        pltpu.VMEM((V_DIM + ONES_ROWS, cols), F32)]
        + [pltpu.VMEM((tq, cols), F32)] * SCORE_BUFFERS,
        compiler_params=_params(("parallel", "parallel"), 40),
        name="attn_prompt",
    )(*lams, sg_col, qb, kb, vt)


def _ssd_pre_body(xbc_ref, s0_ref, s1_ref, s2_ref, dt_ref, cw_ref, cb_ref, dtb_ref, a_ref, e_ref,
                  xs_ref, bm_ref, cm_ref, xdt_ref, dec_ref, *, d_inner):
    cw = cw_ref[...]
    conv = (cb_ref[...] + cw[0:1] * s0_ref[...] + cw[1:2] * s1_ref[...] + cw[2:3] * s2_ref[...]
            + cw[3:4] * xbc_ref[...])
    act = _silu(conv)
    xs = act[:, :d_inner]
    dt = _softplus(dt_ref[...] + dtb_ref[...])
    e = e_ref[...]
    dt_e = _select_matmul(dt, e)
    dec_e = _select_matmul(jnp.exp(dt * a_ref[...]), e)
    xs_ref[...] = xs
    bm_ref[...] = act[:, d_inner:d_inner + LANES]
    cm_ref[...] = act[:, d_inner + LANES:d_inner + 2 * LANES]
    xdt_ref[...] = xs * dt_e
    dec_ref[...] = dec_e


def _ssd_sample_pre(xbc, s0, s1, s2, dt, cw, cb, dtb, a, e):
    n, d_inner = xbc.shape[0], e.shape[1]
    wide = jax.ShapeDtypeStruct((n, d_inner), F32)
    narrow = jax.ShapeDtypeStruct((n, LANES), F32)
    return pl.pallas_call(
        functools.partial(_ssd_pre_body, d_inner=d_inner),
        out_shape=[wide, narrow, narrow, wide, wide],
        compiler_params=pltpu.CompilerParams(vmem_limit_bytes=32 * MIB),
        name="ssd_sample_pre",
    )(xbc, s0, s1, s2, dt, cw, cb, dtb, a, e)


def _ssd_step_body(h0_ref, xdt_ref, dec_ref, bm_ref, cm_ref, xs_ref, z_ref, dsk_ref, ng_ref,
                   hn_ref, y_ref, *, nh_g):
    xdt, dec, bm, cm = xdt_ref[0], dec_ref[0], bm_ref[0], cm_ref[0]
    krows = 2 * SUBLANES

    def tile_rows(rows):
        w = rows[0].shape[1]
        return jnp.concatenate([r.astype(BF16) for r in rows] + [jnp.zeros((krows - len(rows), w), BF16)], axis=0)

    ones3 = tile_rows([jnp.ones((1, SSM_STATE), F32)] * 3)
    ys = []
    for g in range(SSM_GROUPS):
        h0 = h0_ref[0, 0, g].reshape(nh_g, SSM_STATE)
        bg = bm[:, g * SSM_STATE:(g + 1) * SSM_STATE]
        cg = cm[:, g * SSM_STATE:(g + 1) * SSM_STATE]
        xg = xdt[:, g * nh_g:(g + 1) * nh_g]
        dg = dec[:, g * nh_g:(g + 1) * nh_g]
        outer = lax.dot_general(tile_rows([xg]), tile_rows([bg]), TN_DIMS, preferred_element_type=F32)
        dcol = lax.dot_general(tile_rows(list(_split3(dg))), ones3, TN_DIMS, preferred_element_type=F32)
        hn_ref[0, g] = (h0 * dcol + outer).reshape(hn_ref.shape[2:])
        ch0 = lax.dot_general(tile_rows([cg]), h0.astype(BF16), NT_DIMS, preferred_element_type=F32)[0:1]
        ys.append(dg * ch0 + xg * jnp.sum(cg * bg, axis=-1, keepdims=True))
    y = jnp.concatenate(ys, axis=1) + xs_ref[0] * dsk_ref[...]
    yg = y * _silu(z_ref[0])
    outs = [_rms(yg[:, g * nh_g:(g + 1) * nh_g]) for g in range(SSM_GROUPS)]
    y_ref[0] = (jnp.concatenate(outs, axis=1) * ng_ref[...]).astype(BF16)


def _ssd_sample_step(state, layer, xdt, dec, bm, cm, xs, z, dsk, ng):
    n = state.shape[1]
    st_dims = state.shape[2:]
    d_inner = xs.shape[-1]
    row3 = lambda w: pl.BlockSpec((1, 1, w), lambda b: (b, 0, 0))
    r3 = lambda t: t.reshape(n, 1, t.shape[-1])
    return pl.pallas_call(
        functools.partial(_ssd_step_body, nh_g=d_inner // SSM_GROUPS),
        grid=(n,),
        in_specs=[pl.BlockSpec((1, 1) + st_dims, lambda b: (layer, b, 0, 0, 0, 0)),
                  row3(d_inner), row3(d_inner), row3(LANES), row3(LANES), row3(d_inner), row3(d_inner),
                  _resident(dsk.shape), _resident(ng.shape)],
        out_specs=[pl.BlockSpec((1,) + st_dims, lambda b: (b, 0, 0, 0, 0)), row3(d_inner)],
        out_shape=[jax.ShapeDtypeStruct((n,) + st_dims, F32), jax.ShapeDtypeStruct((n, 1, d_inner), BF16)],
        compiler_params=_params(("parallel",), 32),
        name="ssd_sample_step",
    )(state, r3(xdt), r3(dec), r3(bm), r3(cm), r3(xs), r3(z), dsk, ng)


def _attn_sample_body(pt_ref, lq1_ref, lk1_ref, lq2_ref, lk2_ref, sg_ref, q_ref, kn_ref, vn_ref, *rest,
                      pages, lam_init):
    k_refs, v_refs = rest[:pages], rest[pages:2 * pages]
    o_ref, qp_scr, bias_scr, m_scr, l_scr, acc_scr = rest[2 * pages:]
    j = pl.program_id(1)
    prow = PAGE_SIZE * KV_HEADS

    def head_rows(row128):
        per_head = [jnp.broadcast_to(row128[:, h * LANES:(h + 1) * LANES], (2, LANES)) for h in range(KV_HEADS)]
        return jnp.concatenate(per_head * KV_REP, axis=0)

    @pl.when(j == 0)
    def _():
        q = q_ref[0].astype(F32)
        lane = lax.broadcasted_iota(jnp.int32, (2, LANES), 1)
        comp = lax.broadcasted_iota(jnp.int32, (2, LANES), 0)
        own_half = (lane // QK_DIM) == comp
        rows = []
        for r in range(KV_REP):
            for h in range(KV_HEADS):
                qhr = q[:, (KV_REP * h + r) * LANES:(KV_REP * h + r + 1) * LANES]
                rows.append(jnp.where(own_half, jnp.broadcast_to(qhr, (2, LANES)), 0.0))
        qp_scr[...] = jnp.concatenate(rows, axis=0).astype(BF16)
        row_head = (lax.broadcasted_iota(jnp.int32, bias_scr.shape, 0) // 2) % KV_HEADS
        col_head = lax.broadcasted_iota(jnp.int32, bias_scr.shape, 1) % KV_HEADS
        bias_scr[...] = jnp.where(row_head == col_head, 0.0, -jnp.inf)
        m_scr[...] = jnp.full(m_scr.shape, -jnp.inf, F32)
        l_scr[...] = jnp.zeros(l_scr.shape, F32)
        acc_scr[...] = jnp.zeros(acc_scr.shape, F32)

    qp = qp_scr[...]
    s = jnp.concatenate(
        [lax.dot_general(qp, k_refs[i][...].astype(BF16), NT_DIMS, preferred_element_type=F32)
         for i in range(pages)], axis=1) + bias_scr[...]
    m_prev = m_scr[...]
    m_new = jnp.maximum(m_prev, jnp.max(s, axis=-1, keepdims=True))
    alpha = jnp.exp2(m_prev - m_new)
    p = jnp.exp2(s - m_new)
    l_scr[...] = alpha * l_scr[...] + jnp.sum(p, axis=-1, keepdims=True)
    pb = p.astype(BF16)
    pv = jnp.dot(pb[:, 0:prow], v_refs[0][...].astype(BF16), preferred_element_type=F32)
    for i in range(1, pages):
        pv = pv + jnp.dot(pb[:, i * prow:(i + 1) * prow], v_refs[i][...].astype(BF16),
                          preferred_element_type=F32)
    acc_scr[...] = alpha * acc_scr[...] + pv
    m_scr[...] = m_new

    @pl.when(j == pl.num_programs(1) - 1)
    def _():
        k_self = head_rows(kn_ref[0]).astype(BF16).astype(F32)
        s_self = jnp.sum(qp.astype(F32) * k_self, axis=-1, keepdims=True)
        m_prev = m_scr[...]
        m_fin = jnp.maximum(m_prev, s_self)
        alpha = jnp.exp2(m_prev - m_fin)
        p_self = jnp.exp2(s_self - m_fin)
        l_fin = alpha * l_scr[...] + p_self
        acc = (alpha * acc_scr[...] + p_self * head_rows(vn_ref[0])) / l_fin
        lam = _lambda(lq1_ref, lk1_ref, lq2_ref, lk2_ref, lam_init)
        for h in range(KV_HEADS):
            for r in range(KV_REP):
                i1 = (r * KV_HEADS + h) * 2
                o = acc[i1:i1 + 1, :] - lam * acc[i1 + 1:i1 + 2, :]
                y = _rms(o) * sg_ref[...] * (1.0 - lam_init)
                dst = (KV_REP * h + r) * LANES
                o_ref[0, :, dst:dst + LANES] = y.astype(BF16)


def _attn_sample(page_table, lams, sg, qb, k_new, v_new, cache_k, cache_v, layer, lam_init):
    n, n_pages = page_table.shape
    pages = PAGES_PER_STEP
    while n_pages % pages:
        pages //= 2
    att_w = qb.shape[-1]
    kw = KV_HEADS * LANES
    nrow = 2 * KV_REP * KV_HEADS
    n_pool = cache_k.shape[1]
    prow = PAGE_SIZE * KV_HEADS
    ck = cache_k.reshape(cache_k.shape[0] * n_pool * prow, 2 * QK_DIM)
    cv = cache_v.reshape(cache_v.shape[0] * n_pool * prow, V_DIM)
    page_table = page_table + layer * n_pool
    const = lambda shape: pl.BlockSpec(shape, lambda b, j, pt: (0,) * len(shape))
    row3 = lambda w: pl.BlockSpec((1, 1, w), lambda b, j, pt: (b, 0, 0))

    def page_spec(i):
        return pl.BlockSpec((prow, LANES), lambda b, j, pt: (pt[b * n_pages + j * pages + i], 0))

    grid_spec = pltpu.PrefetchScalarGridSpec(
        num_scalar_prefetch=1,
        grid=(n, n_pages // pages),
        in_specs=[const(t.shape) for t in lams] + [const(sg.shape), row3(att_w), row3(kw), row3(kw)]
        + [page_spec(i) for i in range(pages)] + [page_spec(i) for i in range(pages)],
        out_specs=row3(att_w),
        scratch_shapes=[pltpu.VMEM((nrow, LANES), BF16), pltpu.VMEM((nrow, pages * prow), F32),
                        pltpu.VMEM((nrow, 1), F32), pltpu.VMEM((nrow, 1), F32), pltpu.VMEM((nrow, LANES), F32)],
    )
    return pl.pallas_call(
        functools.partial(_attn_sample_body, pages=pages, lam_init=lam_init),
        grid_spec=grid_spec,
        out_shape=jax.ShapeDtypeStruct((n, 1, att_w), BF16),
        compiler_params=_params(("parallel", "arbitrary"), 48),
        name="attn_sample",
    )(page_table.reshape(-1), *lams, sg, qb.reshape(n, 1, att_w), k_new.reshape(n, 1, kw),
      v_new.reshape(n, 1, kw), *([ck] * pages), *([cv] * pages))


def _merge_residual_norm(ys_ref, ya_ref, gs_ref, ga_ref, x_ref, wso_ref, wao_ref, wo_ref, g_ref):
    o_ssd = jnp.dot(ys_ref[...], wso_ref[...], preferred_element_type=F32)
    o_att = jnp.dot(ya_ref[...], wao_ref[...], preferred_element_type=F32)
    merged = _sigmoid(gs_ref[...]) * o_ssd + _sigmoid(ga_ref[...]) * o_att
    x1 = x_ref[...] + jnp.dot(merged.astype(BF16), wo_ref[...], preferred_element_type=F32)
    return x1, (_rms(x1) * g_ref[...]).astype(BF16)


def _outproj_body(ys_ref, ya_ref, gs_ref, ga_ref, x_ref, wso_ref, wao_ref, wo_ref, g_ref, x1_ref, hn_ref):
    x1, hn = _merge_residual_norm(ys_ref, ya_ref, gs_ref, ga_ref, x_ref, wso_ref, wao_ref, wo_ref, g_ref)
    x1_ref[...] = x1
    hn_ref[...] = hn


def _out_proj(ys, ya, gs, ga, x, wso, wao, wo, g, tm):
    rows, d = x.shape
    row_spec = lambda n: pl.BlockSpec((tm, n), lambda i: (i, 0))
    return pl.pallas_call(
        _outproj_body,
        grid=(rows // tm,),
        in_specs=[row_spec(ys.shape[1]), row_spec(ya.shape[1]), row_spec(d), row_spec(d), row_spec(d)]
        + [_resident(t.shape) for t in (wso, wao, wo, g)],
        out_specs=[row_spec(d), row_spec(d)],
        out_shape=[jax.ShapeDtypeStruct((rows, d), F32), jax.ShapeDtypeStruct((rows, d), BF16)],
        compiler_params=_params(("parallel",), 48),
        name="out_proj",
    )(ys, ya, gs, ga, x, wso, wao, wo, g)


def _ffn_chunks(dff):
    tf = 2 * LANES
    assert dff % tf == 0
    return tf, dff // tf


def _mixer_ffn_body(ys_ref, ya_ref, gs_ref, ga_ref, x_ref, wso_ref, wao_ref, wo_ref, g_ref,
                    wup_ref, wdn_ref, cw_ref, cb_ref, y_ref, tail_ref, ubuf_scr, act_scr, *, tm, dff):
    t = pl.program_id(1)
    pad = SUBLANES

    @pl.when(t == 0)
    def _():
        ubuf_scr[0:pad, :] = jnp.zeros((pad, ubuf_scr.shape[1]), F32)

    x1, hn = _merge_residual_norm(ys_ref, ya_ref, gs_ref, ga_ref, x_ref, wso_ref, wao_ref, wo_ref, g_ref)
    tf, nchunk = _ffn_chunks(dff)
    for c in range(nchunk):
        halves = []
        for half in range(2):
            sl = slice(half * dff + c * tf, half * dff + (c + 1) * tf)
            u = jnp.dot(hn, wup_ref[:, sl], preferred_element_type=F32)
            ubuf_scr[pad:pad + tm, sl] = u
            cw = cw_ref[:, sl]
            conv = cb_ref[:, sl] + cw[FFN_CONV - 1:FFN_CONV] * u
            for kk in range(FFN_CONV - 1):
                start = pad - (FFN_CONV - 1) + kk
                conv = conv + cw[kk:kk + 1] * ubuf_scr[start:start + tm, sl]
            ubuf_scr[0:pad, sl] = ubuf_scr[tm:tm + pad, sl]
            halves.append(conv)
        act_scr[:, c * tf:(c + 1) * tf] = (_silu(halves[1]) * halves[0]).astype(BF16)
    y_ref[...] = x1 + jnp.dot(act_scr[...], wdn_ref[...], preferred_element_type=F32)

    @pl.when(t == pl.num_programs(1) - 1)
    def _():
        tail_ref[0] = ubuf_scr[0:pad, :]


def _mixer_ffn_prompt(ys, ya, gs, ga, x, wso, wao, wo, g, wup, wdn, cw, cb, batch, seq, tm):
    d = x.shape[1]
    dff = wdn.shape[0]
    nt = seq // tm
    row_spec = lambda n: pl.BlockSpec((tm, n), lambda b, t: (b * nt + t, 0))
    return pl.pallas_call(
        functools.partial(_mixer_ffn_body, tm=tm, dff=dff),
        grid=(batch, nt),
        in_specs=[row_spec(ys.shape[1]), row_spec(ya.shape[1]), row_spec(d), row_spec(d), row_spec(d)]
        + [_resident(t.shape) for t in (wso, wao, wo, g, wup, wdn, cw, cb)],
        out_specs=[row_spec(d), pl.BlockSpec((1, SUBLANES, 2 * dff), lambda b, t: (b, 0, 0))],
        out_shape=[jax.ShapeDtypeStruct(x.shape, F32), jax.ShapeDtypeStruct((batch, SUBLANES, 2 * dff), F32)],
        scratch_shapes=[pltpu.VMEM((SUBLANES + tm, 2 * dff), F32), pltpu.VMEM((tm, dff), BF16)],
        compiler_params=_params(("parallel", "arbitrary"), 56),
        name="mixer_ffn_prompt",
    )(ys, ya, gs, ga, x, wso, wao, wo, g, wup, wdn, cw, cb)


def _ffn_sample_body(hn_ref, x1_ref, s0_ref, s1_ref, wup_ref, wdn_ref, cw_ref, cb_ref, y_ref, u_ref, *, dff):
    hn = hn_ref[...]
    tf, nchunk = _ffn_chunks(dff)
    acc = jnp.zeros(y_ref.shape, F32)
    for c in range(nchunk):
        halves = []
        for half in range(2):
            sl = slice(half * dff + c * tf, half * dff + (c + 1) * tf)
            u = jnp.dot(hn, wup_ref[:, sl], preferred_element_type=F32)
            u_ref[:, sl] = u
            cw = cw_ref[:, sl]
            halves.append(cb_ref[:, sl] + cw[0:1] * s0_ref[:, sl] + cw[1:2] * s1_ref[:, sl] + cw[2:3] * u)
        act = (_silu(halves[1]) * halves[0]).astype(BF16)
        acc = acc + jnp.dot(act, wdn_ref[c * tf:(c + 1) * tf, :], preferred_element_type=F32)
    y_ref[...] = x1_ref[...] + acc


def _ffn_sample(hn, x1, s0, s1, wup, wdn, cw, cb):
    dff = wdn.shape[0]
    return pl.pallas_call(
        functools.partial(_ffn_sample_body, dff=dff),
        out_shape=[jax.ShapeDtypeStruct(x1.shape, F32), jax.ShapeDtypeStruct((x1.shape[0], 2 * dff), F32)],
        compiler_params=pltpu.CompilerParams(vmem_limit_bytes=52 * MIB),
        name="ffn_sample",
    )(hn, x1, s0, s1, wup, wdn, cw, cb)


def _largest_tile(n, cap):
    t = cap
    while n % t:
        t //= 2
    return t


def kernel(x_prompt, x_sample, cache_k, cache_v, state_conv, state_ssm, state_ffn_conv, page_table, norm_mix_g, w_in, ssm_conv_w, ssm_conv_b, ssm_dt_bias, ssm_a_log, ssm_d, ssm_norm_g, q_norm_g, k_norm_g, lambda_q1, lambda_k1, lambda_q2, lambda_k2, attn_subln_g, w_ssd_o, w_attn_o, w_o, norm_ffn_g, w_ffn_up, ffn_conv_w, ffn_conv_b, w_ffn_down):
    bp, sp, d = x_prompt.shape
    bs, ss, _ = x_sample.shape
    assert ss == 1, "the sample group decodes one token per sequence"
    depth = w_in.shape[0]
    past = page_table.shape[1] * PAGE_SIZE
    d_inner = w_ssd_o.shape[1]
    n_heads = ssm_dt_bias.shape[1]
    conv_dim = ssm_conv_w.shape[2]
    att_w = w_attn_o.shape[1]
    kw = KV_HEADS * 2 * QK_DIM
    vw = KV_HEADS * V_DIM
    dff = w_ffn_down.shape[1]
    heads_g = n_heads // SSM_GROUPS
    nh_g = d_inner // SSM_GROUPS

    sizes = (d_inner, conv_dim, n_heads, att_w, kw, vw, d, d)
    cuts = np.concatenate([[0], np.cumsum(sizes)])
    order = (0, 1, 3, 4, 5, 6, 7, 2)
    widths = tuple(sizes[i] for i in order[:-1]) + (LANES,)

    tables_p = _rope_tables(jnp.arange(sp))
    tables_s = _rope_tables(jnp.broadcast_to(past + jnp.arange(ss), (bs,)))
    bd = jnp.asarray(np.kron(np.eye(LANES // QK_DIM), np.ones((QK_DIM, QK_DIM))), BF16)
    expand = jnp.asarray(np.tile(np.pad(np.kron(np.eye(n_heads), np.ones((1, SSM_HEAD_DIM))),
                                        ((0, LANES - n_heads), (0, 0))), (3, 1)), BF16)
    pad_heads = lambda v: jnp.pad(v.astype(F32), (0, LANES - n_heads)).reshape(1, LANES)

    xp = x_prompt.reshape(bp * sp, d)
    xs_ = x_sample.reshape(bs * ss, d)
    tm_p = _largest_tile(sp, 256)
    tq = _largest_tile(sp, 256)
    tm_f = _largest_tile(sp, 256)

    outs = [[] for _ in range(10)]
    for layer in range(depth):
        lam_init = 0.8 - 0.6 * math.exp(-0.3 * layer)
        wl = w_in[layer]
        w_perm = jnp.concatenate([wl[:, cuts[i]:cuts[i + 1]] for i in order]
                                 + [jnp.zeros((d, LANES - n_heads), F32)], axis=1).astype(BF16)
        g_mix = norm_mix_g[layer].reshape(1, d)
        cw, cb = ssm_conv_w[layer], ssm_conv_b[layer].reshape(1, conv_dim)
        dtb = pad_heads(ssm_dt_bias[layer])
        a_neg = pad_heads(-jnp.exp(ssm_a_log[layer].astype(F32)))
        dsk = jnp.repeat(ssm_d[layer].astype(F32), SSM_HEAD_DIM).reshape(1, d_inner)
        ng = ssm_norm_g[layer].reshape(1, d_inner)
        qg = jnp.tile(q_norm_g[layer], LANES // QK_DIM).reshape(1, LANES)
        kg = jnp.tile(k_norm_g[layer], LANES // QK_DIM).reshape(1, LANES)
        lams = tuple(t[layer].reshape(1, QK_DIM) for t in (lambda_q1, lambda_k1, lambda_q2, lambda_k2))
        sg = attn_subln_g[layer].reshape(1, V_DIM)
        wso, wao, wo = (t[layer].astype(BF16) for t in (w_ssd_o, w_attn_o, w_o))
        g_ffn = norm_ffn_g[layer].reshape(1, d)
        wup, wdn = w_ffn_up[layer].astype(BF16), w_ffn_down[layer].astype(BF16)
        fcw, fcb = ffn_conv_w[layer], ffn_conv_b[layer].reshape(1, 2 * dff)

        z, xbc, qb, kf, kb, v, vt, gs, ga, dt = _in_proj(xp, g_mix, w_perm, tables_p, qg, kg, bd, widths, tm_p, sp)
        y_ssd, ctail, st = _ssd_prompt(z, xbc, dt, cw, cb, dtb, a_neg, expand, dsk, ng, bp, sp)
        vt = vt.reshape(bp * KV_HEADS * (V_DIM + ONES_ROWS), sp)
        y_att = _attn_prompt(lams, sg.reshape(V_DIM, 1), qb, kb, vt, bp, sp, lam_init, tq)
        xp, ftail = _mixer_ffn_prompt(y_ssd, y_att, gs, ga, xp, wso, wao, wo, g_ffn, wup, wdn, fcw, fcb,
                                      bp, sp, tm_f)
        outs[0].append(kf.reshape(bp, sp, KV_HEADS, 2 * QK_DIM))
        outs[1].append(v.reshape(bp, sp, KV_HEADS, V_DIM))
        outs[2].append(ctail[:, SUBLANES - (SSM_CONV - 1):, :])
        outs[3].append(st.reshape(bp, SSM_GROUPS, SSM_STATE, heads_g, SSM_HEAD_DIM).transpose(0, 1, 3, 4, 2))
        outs[4].append(ftail[:, SUBLANES - (FFN_CONV - 1):, :])

        z, xbc, qb, kf, _, v, _, gs, ga, dt = _in_proj(xs_, g_mix, w_perm, tables_s, qg, kg, bd, widths, bs, bs)
        sc = state_conv[layer]
        xs_c, bm, cm, xdt, dec = _ssd_sample_pre(xbc, sc[:, 0], sc[:, 1], sc[:, 2], dt, cw, cb, dtb, a_neg, expand)
        h_new, y_ssd = _ssd_sample_step(state_ssm, layer, xdt, dec, bm, cm, xs_c, z, dsk, ng)
        y_att = _attn_sample(page_table, lams, sg, qb, kf, v, cache_k, cache_v, layer, lam_init)
        x1, hn = _out_proj(y_ssd.reshape(bs, d_inner), y_att.reshape(bs, att_w), gs, ga, xs_, wso, wao, wo, g_ffn, bs)
        sf = state_ffn_conv[layer]
        xs_, u_new = _ffn_sample(hn, x1, sf[:, 0], sf[:, 1], wup, wdn, fcw, fcb)
        outs[5].append(kf.reshape(bs, ss, KV_HEADS, 2 * QK_DIM))
        outs[6].append(v.reshape(bs, ss, KV_HEADS, V_DIM))
        outs[7].append(jnp.concatenate([sc[:, 1:], xbc[:, None, :]], axis=1))
        outs[8].append(h_new)
        outs[9].append(jnp.concatenate([sf[:, 1:], u_new[:, None, :]], axis=1))

    stacked = [jnp.stack(o) for o in outs]
    return (xp.reshape(bp, sp, d), xs_.reshape(bs, ss, d), *stacked)
```
